```python
import math
import jax
import jax.numpy as jnp
from jax import lax
import numpy as np

D_MODEL = 4096
BATCH = 4
SEQ = 2048
DEPTH = 2
DEC_BATCH = 8
DEC_SEQ = 1
PAST_LEN = 16384
PAGE_SIZE = 128

RWKV_WIDTH = D_MODEL // 2
RWKV_HEAD_DIM = 64
RWKV_HEADS = RWKV_WIDTH // RWKV_HEAD_DIM
RWKV_DECAY_LORA = max(32, int(round(1.8 * RWKV_WIDTH ** 0.5 / 32)) * 32)
RWKV_AAA_LORA = max(32, int(round(1.8 * RWKV_WIDTH ** 0.5 / 32)) * 32)
RWKV_GATE_LORA = max(32, int(round(0.6 * RWKV_WIDTH ** 0.8 / 32)) * 32)
RWKV_PROJ = 3 * RWKV_WIDTH + RWKV_DECAY_LORA + RWKV_AAA_LORA + RWKV_GATE_LORA
RWKV_GN_EPS = 64e-5
DIFF_WIDTH = D_MODEL - RWKV_WIDTH
DIFF_HEADS = 8
DIFF_HEAD_DIM = DIFF_WIDTH // (2 * DIFF_HEADS)
DIFF_PROJ = 3 * DIFF_WIDTH
DIFF_LAYER_INDEX = 0
DIFF_LAMBDA_INIT = 0.8 - 0.6 * math.exp(-0.3 * DIFF_LAYER_INDEX)
SUBLN_EPS = 1e-5
MOBA_HEADS = 32
MOBA_HEAD_DIM = D_MODEL // MOBA_HEADS
MOBA_BLOCK = 256
MOBA_TOPK = 3
MOBA_Q_CHUNK = 16
D_FF = 11008
CONV_W = 3
ROPE_THETA = 10000.0
NORM_EPS = 1e-6
ATTN_Q_BLOCK = 128

kernel_name = 'hybrid_rwkv7_diffattn_moba_step'


def rms_norm(x, g, eps=NORM_EPS):
    xf = x.astype(jnp.float32)
    y = xf * lax.rsqrt(jnp.mean(xf * xf, axis=-1, keepdims=True) + eps)
    return (y * g.astype(jnp.float32)).astype(x.dtype)


def rope(x, pos):
    half = x.shape[-1] // 2
    inv = ROPE_THETA ** (-jnp.arange(half, dtype=jnp.float32) / half)
    ang = pos.astype(jnp.float32)[:, None] * inv[None, :]
    cos = jnp.cos(ang)[:, None, :]
    sin = jnp.sin(ang)[:, None, :]
    xf = x.astype(jnp.float32)
    x1, x2 = xf[..., :half], xf[..., half:]
    return jnp.concatenate([x1 * cos - x2 * sin, x2 * cos + x1 * sin], axis=-1).astype(x.dtype)


def rwkv7_mix(p, shift_prev, wkv_prev, mu, w0, w2, a0, a2, g2, k_k, k_a, r_k, ln_w, ln_b):
    B, T, _ = p.shape
    H, N, C = RWKV_HEADS, RWKV_HEAD_DIM, RWKV_WIDTH
    f32 = jnp.float32
    p_prev = jnp.concatenate([shift_prev[:, None, :].astype(p.dtype), p[:, :-1]], axis=1)
    xm = p + (p_prev - p) * mu
    cuts = [C, 2 * C, 3 * C, 3 * C + RWKV_DECAY_LORA, 3 * C + RWKV_DECAY_LORA + RWKV_AAA_LORA]
    r, k, v, wl, al, gl = jnp.split(xm, cuts, axis=-1)
    w = -jax.nn.softplus(-(w0 + jnp.tanh(wl) @ w2)) - 0.5
    a = jax.nn.sigmoid(a0 + al @ a2)
    g = jax.nn.sigmoid(gl) @ g2

    def heads(t):
        return t.astype(f32).reshape(B, T, H, N)

    kk = heads(k * k_k)
    kk = kk / jnp.maximum(jnp.sqrt(jnp.sum(kk * kk, axis=-1, keepdims=True)), 1e-12)
    k = k * (1.0 + (a - 1.0) * k_a)
    rh, kh, vh, ah = heads(r), heads(k), heads(v), heads(a)
    decay = jnp.exp(-jnp.exp(heads(w)))

    def tm(t):
        return jnp.swapaxes(t, 0, 1)

    def step(state, inp):
        r_t, w_t, k_t, v_t, kk_t, a_t = inp
        s_kk = jnp.einsum('bhvk,bhk->bhv', state, kk_t)
        state = (state * w_t[:, :, None, :]
                 - s_kk[..., None] * (kk_t * a_t)[:, :, None, :]
                 + v_t[..., None] * k_t[:, :, None, :])
        return state, jnp.einsum('bhvk,bhk->bhv', state, r_t)

    s_fin, y = lax.scan(step, wkv_prev.astype(f32), (tm(rh), tm(decay), tm(kh), tm(vh), tm(kk), tm(ah)))
    y = tm(y)
    mean = jnp.mean(y, axis=-1, keepdims=True)
    var = jnp.mean(jnp.square(y - mean), axis=-1, keepdims=True)
    y = (y - mean) * lax.rsqrt(var + RWKV_GN_EPS) * ln_w.astype(f32).reshape(H, N) + ln_b.astype(f32).reshape(H, N)
    y = y + jnp.sum(rh * kh * r_k.astype(f32), axis=-1, keepdims=True) * vh
    out = y.reshape(B, T, C).astype(p.dtype) * g
    return out, p[:, -1], s_fin.astype(wkv_prev.dtype)


def diff_lambda(lq1, lk1, lq2, lk2):
    f32 = jnp.float32
    return (jnp.exp(jnp.sum(lq1.astype(f32) * lk1.astype(f32)))
            - jnp.exp(jnp.sum(lq2.astype(f32) * lk2.astype(f32))) + DIFF_LAMBDA_INIT)


def diff_qkv(pd, pos):
    B, T, _ = pd.shape
    q, k, v = jnp.split(pd, [DIFF_WIDTH, 2 * DIFF_WIDTH], axis=-1)
    q = rope(q.reshape(B, T, 2 * DIFF_HEADS, DIFF_HEAD_DIM), pos)
    k = rope(k.reshape(B, T, 2 * DIFF_HEADS, DIFF_HEAD_DIM), pos)
    v = v.reshape(B, T, DIFF_HEADS, 2 * DIFF_HEAD_DIM)
    return q, k, v


def diff_attend(q, k, v, q_pos, k_pos, lam, subln_g):
    B, Tq = q.shape[:2]
    Tk = k.shape[1]
    qh = q.reshape(B, Tq, DIFF_HEADS, 2, DIFF_HEAD_DIM)
    kh = k.reshape(B, Tk, DIFF_HEADS, 2, DIFF_HEAD_DIM)
    s = jnp.einsum('bqhcd,bkhcd->bhcqk', qh, kh).astype(jnp.float32) * (DIFF_HEAD_DIM ** -0.5)
    s = jnp.where(k_pos[None, :] <= q_pos[:, None], s, -jnp.inf)
    pr = jax.nn.softmax(s, axis=-1)
    attn = pr[:, :, 0] - lam * pr[:, :, 1]
    o = jnp.einsum('bhqk,bkhe->bqhe', attn.astype(v.dtype), v)
    return rms_norm(o, subln_g, SUBLN_EPS) * (1.0 - DIFF_LAMBDA_INIT)


def diff_prompt(q, k, v, pos, lam, subln_g):
    B, S = q.shape[:2]
    nq = S // ATTN_Q_BLOCK
    qb = jnp.swapaxes(q.reshape(B, nq, ATTN_Q_BLOCK, 2 * DIFF_HEADS, DIFF_HEAD_DIM), 0, 1)
    pb = pos.reshape(nq, ATTN_Q_BLOCK)
    out = lax.map(lambda a: diff_attend(a[0], k, v, a[1], pos, lam, subln_g), (qb, pb))
    return jnp.swapaxes(out, 0, 1).reshape(B, S, DIFF_HEADS, 2 * DIFF_HEAD_DIM)


def diff_sample(q, k, v, cache_k, cache_v, page_table, pos, lam, subln_g):
    DB, T = q.shape[:2]
    n_past = page_table.shape[1] * PAGE_SIZE
    k_past = cache_k[page_table].reshape(DB, n_past, 2 * DIFF_HEADS, DIFF_HEAD_DIM)
    v_past = cache_v[page_table].reshape(DB, n_past, DIFF_HEADS, 2 * DIFF_HEAD_DIM)
    k_all = jnp.concatenate([k_past.astype(k.dtype), k], axis=1)
    v_all = jnp.concatenate([v_past.astype(v.dtype), v], axis=1)
    k_pos = jnp.arange(n_past + T, dtype=jnp.int32)
    return diff_attend(q, k_all, v_all, pos, k_pos, lam, subln_g)


def moba_qkv(pm, pos):
    B, T, _ = pm.shape
    q, k, v = jnp.split(pm, 3, axis=-1)
    shp = (B, T, MOBA_HEADS, MOBA_HEAD_DIM)
    return rope(q.reshape(shp), pos), rope(k.reshape(shp), pos), v.reshape(shp)


def moba_prompt(q, k, v):
    B, S, H, d = q.shape
    BS, QC = MOBA_BLOCK, MOBA_Q_CHUNK
    scale = d ** -0.5
    nb = -(-S // BS)
    pad = ((0, 0), (0, nb * BS - S), (0, 0), (0, 0))
    kb = jnp.pad(k, pad).reshape(B, nb, BS, H, d).transpose(0, 1, 3, 2, 4)
    vb = jnp.pad(v, pad).reshape(B, nb, BS, H, d).transpose(0, 1, 3, 2, 4)
    kmean = jnp.sum(kb.astype(jnp.float32), axis=3) / BS
    pos = jnp.arange(S, dtype=jnp.int32)
    qblk = pos // BS
    gate = jnp.einsum('bshd,bnhd->bshn', q.astype(jnp.float32), kmean)
    eligible = jnp.arange(nb)[None, :] < qblk[:, None]
    gate = jnp.where(eligible[None, :, None, :], gate, -jnp.inf)
    n_sel = min(MOBA_TOPK, nb)
    _, sel = lax.top_k(gate, n_sel)
    valid = sel < qblk[None, :, None, None]
    bi = jnp.arange(B)[:, None, None, None]
    hi = jnp.arange(H)[None, None, :, None]

    def chunk(c):
        q0 = c * QC
        qc = lax.dynamic_slice_in_dim(q, q0, QC, axis=1)
        selc = lax.dynamic_slice_in_dim(sel, q0, QC, axis=1)
        validc = lax.dynamic_slice_in_dim(valid, q0, QC, axis=1)
        posc = q0 + jnp.arange(QC, dtype=jnp.int32)
        own = q0 // BS
        ko = lax.dynamic_index_in_dim(kb, own, axis=1, keepdims=False)
        vo = lax.dynamic_index_in_dim(vb, own, axis=1, keepdims=False)
        kg = kb[bi, selc, hi]
        vg = vb[bi, selc, hi]
        s_sel = jnp.einsum('bqhd,bqhjtd->bqhjt', qc, kg).astype(jnp.float32) * scale
        s_sel = jnp.where(validc[..., None], s_sel, -jnp.inf)
        s_own = jnp.einsum('bqhd,bhtd->bqht', qc, ko).astype(jnp.float32) * scale
        kpos_own = own * BS + jnp.arange(BS, dtype=jnp.int32)
        s_own = jnp.where((kpos_own[None, :] <= posc[:, None])[None, :, None, :], s_own, -jnp.inf)
        p = jax.nn.softmax(jnp.concatenate([s_sel.reshape(B, QC, H, n_sel * BS), s_own], axis=-1), axis=-1)
        p_sel = p[..., :n_sel * BS].reshape(B, QC, H, n_sel, BS).astype(v.dtype)
        p_own = p[..., n_sel * BS:].astype(v.dtype)
        return (jnp.einsum('bqhjt,bqhjte->bqhe', p_sel, vg)
                + jnp.einsum('bqht,bhte->bqhe', p_own, vo))

    out = lax.map(chunk, jnp.arange(S // QC, dtype=jnp.int32))
    return jnp.swapaxes(out, 0, 1).reshape(B, S, H, d)


def moba_sample(q, k, v, cache_k, cache_v, page_table):
    DB, T, H, d = q.shape
    BS = MOBA_BLOCK
    ppb = BS // PAGE_SIZE
    scale = d ** -0.5
    n_pages = page_table.shape[1]
    n_past = n_pages * PAGE_SIZE
    nb = -(-(n_past + T) // BS)
    nbp = -(-n_pages // ppb)
    page_sum = jnp.sum(cache_k[page_table].astype(jnp.float32), axis=2)
    page_sum = jnp.pad(page_sum, ((0, 0), (0, nbp * ppb - n_pages), (0, 0), (0, 0)))
    blk_sum = jnp.pad(page_sum.reshape(DB, nbp, ppb, H, d).sum(axis=2), ((0, 0), (0, nb - nbp), (0, 0), (0, 0)))
    pos = PAST_LEN + jnp.arange(T, dtype=jnp.int32)
    qblk = pos // BS
    new_sum = jax.ops.segment_sum(jnp.swapaxes(k.astype(jnp.float32), 0, 1), qblk, num_segments=nb)
    kmean = (blk_sum + jnp.swapaxes(new_sum, 0, 1)) / BS
    gate = jnp.einsum('bthd,bnhd->bthn', q.astype(jnp.float32), kmean)
    eligible = jnp.arange(nb)[None, :] < qblk[:, None]
    gate = jnp.where(eligible[None, :, None, :], gate, -jnp.inf)
    n_sel = min(MOBA_TOPK, nb)
    _, sel = lax.top_k(gate, n_sel)
    valid = sel < qblk[None, :, None, None]
    J = n_sel + 1
    idx = jnp.concatenate([sel, jnp.broadcast_to(qblk[None, :, None, None], (DB, T, H, 1))], axis=-1)
    valid_all = jnp.concatenate([valid, jnp.ones((DB, T, H, 1), dtype=bool)], axis=-1)
    lp = idx[..., None] * ppb + jnp.arange(ppb, dtype=jnp.int32)
    bi = jnp.arange(DB)[:, None, None, None, None]
    phys = page_table[bi, jnp.clip(lp, 0, n_pages - 1)]
    hi = jnp.arange(H)[None, None, :, None, None]
    kg = cache_k[phys, :, hi].reshape(DB, T, H, J, BS, d).astype(q.dtype)
    vg = cache_v[phys, :, hi].reshape(DB, T, H, J, BS, d).astype(v.dtype)
    kpos = idx[..., None] * BS + jnp.arange(BS, dtype=jnp.int32)
    s_past = jnp.einsum('bqhd,bqhjtd->bqhjt', q, kg).astype(jnp.float32) * scale
    s_past = jnp.where(valid_all[..., None] & (kpos < n_past), s_past, -jnp.inf)
    s_new = jnp.einsum('bqhd,bkhd->bqhk', q, k).astype(jnp.float32) * scale
    in_sel = jnp.any((sel[..., None] == qblk[None, None, None, None, :]) & valid[..., None], axis=3)
    own_k = (qblk[None, :] == qblk[:, None])[None, :, None, :]
    causal = (pos[None, :] <= pos[:, None])[None, :, None, :]
    s_new = jnp.where((in_sel | own_k) & causal, s_new, -jnp.inf)
    p = jax.nn.softmax(jnp.concatenate([s_past.reshape(DB, T, H, J * BS), s_new], axis=-1), axis=-1)
    p_past = p[..., :J * BS].reshape(DB, T, H, J, BS).astype(v.dtype)
    p_new = p[..., J * BS:].astype(v.dtype)
    return (jnp.einsum('bqhjt,bqhjte->bqhe', p_past, vg)
            + jnp.einsum('bqhk,bkhe->bqhe', p_new, v))


def conv_ffn(h, conv_prev, w_up, conv_w, conv_b, w_down):
    T = h.shape[1]
    gate, up = jnp.split(h @ w_up, 2, axis=-1)
    gcat = jnp.concatenate([conv_prev.astype(gate.dtype), gate], axis=1)
    gc = conv_b
    for i in range(CONV_W):
        gc = gc + conv_w[i] * gcat[:, i:i + T]
    y = (jax.nn.silu(gc) * up) @ w_down
    return y, gcat[:, T:]


def setup_inputs(seed: int = 0) -> dict:
    key = jax.random.key(seed)
    ks = jax.random.split(key, 40)
    f32 = jnp.float32

    def nrm(i, shape, scale=1.0):
        return scale * jax.random.normal(ks[i], shape, f32)

    n_pages = PAST_LEN // PAGE_SIZE
    n_pool = (DEC_BATCH * n_pages * 5 + 3) // 4
    page_table = jax.random.permutation(ks[6], n_pool)[:DEC_BATCH * n_pages].reshape(DEC_BATCH, n_pages).astype(jnp.int32)
    ds = D_MODEL ** -0.5
    C = RWKV_WIDTH
    return {
        'x_prompt': nrm(0, (BATCH, SEQ, D_MODEL)),
        'x_sample': nrm(1, (DEC_BATCH, DEC_SEQ, D_MODEL)),
        'cache_k0': nrm(2, (n_pool, PAGE_SIZE, 2 * DIFF_HEADS, DIFF_HEAD_DIM)),
        'cache_v0': nrm(3, (n_pool, PAGE_SIZE, DIFF_HEADS, 2 * DIFF_HEAD_DIM)),
        'cache_k1': nrm(4, (n_pool, PAGE_SIZE, MOBA_HEADS, MOBA_HEAD_DIM)),
        'cache_v1': nrm(5, (n_pool, PAGE_SIZE, MOBA_HEADS, MOBA_HEAD_DIM)),
        'page_table': page_table,
        'state_wkv0': nrm(7, (DEC_BATCH, RWKV_HEADS, RWKV_HEAD_DIM, RWKV_HEAD_DIM), 0.1),
        'state_shift0': nrm(8, (DEC_BATCH, RWKV_PROJ)),
        'state_conv': nrm(9, (DEPTH, DEC_BATCH, CONV_W - 1, D_FF)),
        'norm_mix': 1.0 + nrm(10, (DEPTH, D_MODEL), 0.02),
        'norm_ffn': 1.0 + nrm(11, (DEPTH, D_MODEL), 0.02),
        'norm_final': 1.0 + nrm(12, (D_MODEL,), 0.02),
        'w_in0': nrm(13, (D_MODEL, RWKV_PROJ + DIFF_PROJ), ds),
        'w_out0': nrm(14, (D_MODEL, D_MODEL), ds),
        'rwkv_mu': jax.random.uniform(ks[15], (RWKV_PROJ,), f32),
        'rwkv_w0': nrm(16, (C,), 0.5),
        'rwkv_w2': nrm(17, (RWKV_DECAY_LORA, C), RWKV_DECAY_LORA ** -0.5),
        'rwkv_a0': nrm(18, (C,), 0.1),
        'rwkv_a2': nrm(19, (RWKV_AAA_LORA, C), RWKV_AAA_LORA ** -0.5),
        'rwkv_g2': nrm(20, (RWKV_GATE_LORA, C), RWKV_GATE_LORA ** -0.5),
        'rwkv_kk': 0.85 + nrm(21, (C,), 0.02),
        'rwkv_ka': 1.0 + nrm(22, (C,), 0.02),
        'rwkv_rk': nrm(23, (RWKV_HEADS, RWKV_HEAD_DIM), 0.1),
        'rwkv_ln_w': 1.0 + nrm(24, (C,), 0.02),
        'rwkv_ln_b': nrm(25, (C,), 0.02),
        'diff_lq1': nrm(26, (DIFF_HEAD_DIM,), 0.1),
        'diff_lk1': nrm(27, (DIFF_HEAD_DIM,), 0.1),
        'diff_lq2': nrm(28, (DIFF_HEAD_DIM,), 0.1),
        'diff_lk2': nrm(29, (DIFF_HEAD_DIM,), 0.1),
        'diff_subln': 1.0 + nrm(30, (2 * DIFF_HEAD_DIM,), 0.02),
        'w_in1': nrm(31, (D_MODEL, 3 * D_MODEL), ds),
        'w_out1': nrm(32, (D_MODEL, D_MODEL), ds),
        'ffn_w_up': nrm(33, (DEPTH, D_MODEL, 2 * D_FF), ds),
        'ffn_conv_w': nrm(34, (DEPTH, CONV_W, D_FF), CONV_W ** -0.5),
        'ffn_conv_b': nrm(35, (DEPTH, D_FF), 0.02),
        'ffn_w_down': nrm(36, (DEPTH, D_FF, D_MODEL), D_FF ** -0.5),
    }


def reference(x_prompt, x_sample, cache_k0, cache_v0, cache_k1, cache_v1, page_table,
              state_wkv0, state_shift0, state_conv,
              norm_mix, norm_ffn, norm_final, w_in0, w_out0,
              rwkv_mu, rwkv_w0, rwkv_w2, rwkv_a0, rwkv_a2, rwkv_g2, rwkv_kk, rwkv_ka, rwkv_rk,
              rwkv_ln_w, rwkv_ln_b, diff_lq1, diff_lk1, diff_lq2, diff_lk2, diff_subln,
              w_in1, w_out1, ffn_w_up, ffn_conv_w, ffn_conv_b, ffn_w_down):
    B, S, _ = x_prompt.shape
    DB, T, _ = x_sample.shape
    pos_p = jnp.arange(S, dtype=jnp.int32)
    pos_s = PAST_LEN + jnp.arange(T, dtype=jnp.int32)
    rwkv_params = (rwkv_mu, rwkv_w0, rwkv_w2, rwkv_a0, rwkv_a2, rwkv_g2, rwkv_kk, rwkv_ka, rwkv_rk, rwkv_ln_w, rwkv_ln_b)
    lam = diff_lambda(diff_lq1, diff_lk1, diff_lq2, diff_lk2)
    xp, xs = x_prompt, x_sample
    conv_p, conv_s = [], []
    for layer in range(DEPTH):
        hp = rms_norm(xp, norm_mix[layer])
        hs = rms_norm(xs, norm_mix[layer])
        if layer % 2 == 0:
            pp = hp @ w_in0
            ps = hs @ w_in0
            a_p, shift0_p, wkv0_p = rwkv7_mix(
                pp[..., :RWKV_PROJ], jnp.zeros((B, RWKV_PROJ), xp.dtype),
                jnp.zeros((B, RWKV_HEADS, RWKV_HEAD_DIM, RWKV_HEAD_DIM), jnp.float32), *rwkv_params)
            a_s, shift0_s, wkv0_s = rwkv7_mix(ps[..., :RWKV_PROJ], state_shift0, state_wkv0, *rwkv_params)
            q_p, k0_p, v0_p = diff_qkv(pp[..., RWKV_PROJ:], pos_p)
            q_s, k0_s, v0_s = diff_qkv(ps[..., RWKV_PROJ:], pos_s)
            b_p = diff_prompt(q_p, k0_p, v0_p, pos_p, lam, diff_subln)
            b_s = diff_sample(q_s, k0_s, v0_s, cache_k0, cache_v0, page_table, pos_s, lam, diff_subln)
            mp = jnp.concatenate([a_p, b_p.reshape(B, S, DIFF_WIDTH)], axis=-1) @ w_out0
            ms = jnp.concatenate([a_s, b_s.reshape(DB, T, DIFF_WIDTH)], axis=-1) @ w_out0
        else:
            q_p, k1_p, v1_p = moba_qkv(hp @ w_in1, pos_p)
            q_s, k1_s, v1_s = moba_qkv(hs @ w_in1, pos_s)
            mp = moba_prompt(q_p, k1_p, v1_p).reshape(B, S, D_MODEL) @ w_out1
            ms = moba_sample(q_s, k1_s, v1_s, cache_k1, cache_v1, page_table).reshape(DB, T, D_MODEL) @ w_out1
        xp = xp + mp
        xs = xs + ms
        fp, cp = conv_ffn(rms_norm(xp, norm_ffn[layer]), jnp.zeros((B, CONV_W - 1, D_FF), xp.dtype),
                          ffn_w_up[layer], ffn_conv_w[layer], ffn_conv_b[layer], ffn_w_down[layer])
        fs, cs = conv_ffn(rms_norm(xs, norm_ffn[layer]), state_conv[layer],
                          ffn_w_up[layer], ffn_conv_w[layer], ffn_conv_b[layer], ffn_w_down[layer])
        xp = xp + fp
        xs = xs + fs
        conv_p.append(cp)
        conv_s.append(cs)
    y_prompt = rms_norm(xp, norm_final)
    y_sample = rms_norm(xs, norm_final)
    return (y_prompt, y_sample, k0_p, v0_p, k0_s, v0_s, wkv0_p, wkv0_s, shift0_p, shift0_s,
            k1_p, v1_p, k1_s, v1_s, jnp.stack(conv_p), jnp.stack(conv_s))
```

```python
import functools
import math
from typing import NamedTuple

import jax
import jax.numpy as jnp
from jax import lax
from jax.experimental import pallas as pl
from jax.experimental.pallas import tpu as pltpu

F32 = jnp.float32
BF16 = jnp.bfloat16
LANES = 128
SUBLANES = 8
NEG_BIG = -1e30
VMEM_LIMIT = 56 * 1024 * 1024


class Cfg(NamedTuple):
    d_model: int
    batch: int
    seq: int
    dec_batch: int
    past_len: int
    page_size: int
    rwkv_width: int
    rwkv_head_dim: int
    lora_w: int
    lora_a: int
    lora_g: int
    diff_heads: int
    diff_head_dim: int
    moba_heads: int
    moba_head_dim: int
    moba_block: int
    moba_topk: int
    d_ff: int
    rope_theta: float
    norm_eps: float
    subln_eps: float
    gn_eps: float
    lambda_init: float


def _make_cfg(d_model=4096, batch=4, seq=2048, dec_batch=8, past_len=16384, page_size=128, d_ff=11008,
              diff_heads=8, moba_block=256):
    rw = d_model // 2
    lw = max(32, int(round(1.8 * rw ** 0.5 / 32)) * 32)
    lg = max(32, int(round(0.6 * rw ** 0.8 / 32)) * 32)
    return Cfg(d_model=d_model, batch=batch, seq=seq, dec_batch=dec_batch, past_len=past_len,
               page_size=page_size, rwkv_width=rw, rwkv_head_dim=64, lora_w=lw, lora_a=lw, lora_g=lg,
               diff_heads=diff_heads, diff_head_dim=(d_model - rw) // (2 * diff_heads),
               moba_heads=d_model // 128, moba_head_dim=128, moba_block=moba_block, moba_topk=3,
               d_ff=d_ff, rope_theta=10000.0, norm_eps=1e-6, subln_eps=1e-5, gn_eps=64e-5,
               lambda_init=0.8 - 0.6 * math.exp(-0.3 * 0))


_CFG = _make_cfg()


def _rup(n, m):
    return -(-n // m) * m


def _tile(n, pref):
    if n <= pref:
        return n
    t = pref
    while n % t:
        t -= SUBLANES
    return t


def _params(*sem):
    return pltpu.CompilerParams(dimension_semantics=sem, vmem_limit_bytes=VMEM_LIMIT)


def _split_bf16(x):
    hi = x.astype(BF16)
    lo = (x - hi.astype(F32)).astype(BF16)
    return hi, lo


def _seg_dot(x, ind2):
    hi, lo = _split_bf16(x)
    return jnp.dot(jnp.concatenate([hi, lo], axis=1), ind2, preferred_element_type=F32)


def _rms_kernel(x_ref, g_ref, o_ref, *, eps):
    x = x_ref[...]
    ms = jnp.mean(x * x, axis=-1, keepdims=True)
    o_ref[...] = (x * lax.rsqrt(ms + eps) * g_ref[...]).astype(o_ref.dtype)


def _rmsnorm(x, g, eps, out_dtype):
    m, d = x.shape
    tm = _tile(m, 256)
    return pl.pallas_call(
        functools.partial(_rms_kernel, eps=eps),
        out_shape=jax.ShapeDtypeStruct((m, d), out_dtype),
        grid=(m // tm,),
        in_specs=[pl.BlockSpec((tm, d), lambda i: (i, 0)), pl.BlockSpec((1, d), lambda i: (0, 0))],
        out_specs=pl.BlockSpec((tm, d), lambda i: (i, 0)),
        compiler_params=_params("parallel"),
        name="rmsnorm",
    )(x, g.reshape(1, d))


def _mm_kernel(*refs, has_res, has_rope, two_lhs):
    it = iter(refs)
    x_ref = next(it)
    x2_ref = next(it) if two_lhs else None
    w_ref = next(it)
    w2_ref = next(it) if two_lhs else None
    res_ref = next(it) if has_res else None
    cos_ref = next(it) if has_rope else None
    sin_ref = next(it) if has_rope else None
    o_ref = next(it)
    acc = jnp.dot(x_ref[...], w_ref[...], preferred_element_type=F32)
    if two_lhs:
        acc = acc + jnp.dot(x2_ref[...], w2_ref[...], preferred_element_type=F32)
    if has_rope:
        cos = cos_ref[...]
        sin = sin_ref[...]
        for s in range(acc.shape[1] // LANES):
            a = acc[:, s * LANES:(s + 1) * LANES]
            o_ref[:, s * LANES:(s + 1) * LANES] = (
                a * cos + pltpu.roll(a, LANES // 2, axis=1) * sin).astype(o_ref.dtype)
    else:
        if has_res:
            acc = acc + res_ref[...]
        o_ref[...] = acc.astype(o_ref.dtype)


def _matmul(x, w, *, x2=None, w2=None, res=None, rope=None, rows_per_seq=None, out_dtype=F32, tm=1024, tn=512):
    m, k = x.shape
    n = w.shape[1]
    tm = _tile(m if rope is None else rows_per_seq, tm)
    tn = _tile(n, tn)
    two = x2 is not None
    in_specs = [pl.BlockSpec((tm, k), lambda i, j: (i, 0))]
    args = [x]
    if two:
        in_specs.append(pl.BlockSpec((tm, x2.shape[1]), lambda i, j: (i, 0)))
        args.append(x2)
    in_specs.append(pl.BlockSpec((k, tn), lambda i, j: (0, j)))
    args.append(w)
    if two:
        in_specs.append(pl.BlockSpec((w2.shape[0], tn), lambda i, j: (0, j)))
        args.append(w2)
    if res is not None:
        in_specs.append(pl.BlockSpec((tm, tn), lambda i, j: (i, j)))
        args.append(res)
    if rope is not None:
        tps = rows_per_seq // tm
        for t in rope:
            in_specs.append(pl.BlockSpec((tm, LANES), lambda i, j: (i % tps, 0)))
            args.append(t)
    return pl.pallas_call(
        functools.partial(_mm_kernel, has_res=res is not None, has_rope=rope is not None, two_lhs=two),
        out_shape=jax.ShapeDtypeStruct((m, n), out_dtype),
        grid=(m // tm, n // tn),
        in_specs=in_specs,
        out_specs=pl.BlockSpec((tm, tn), lambda i, j: (i, j)),
        compiler_params=_params("parallel", "parallel"),
        name="matmul",
    )(*args)


def _ffn_up_kernel(x_ref, wg_ref, wu_ref, cw_ref, cb_ref, h_ref, cs_ref, carry_ref, *, tiles_per_seq):
    i = pl.program_id(0)
    j = pl.program_id(1)
    x = x_ref[...]
    g = jnp.dot(x, wg_ref[...], preferred_element_type=F32)
    u = jnp.dot(x, wu_ref[...], preferred_element_type=F32)
    tm = g.shape[0]

    @pl.when(i % tiles_per_seq == 0)
    def _():
        carry_ref[j] = jnp.zeros(carry_ref.shape[1:], F32)

    carry = carry_ref[j]
    row = lax.broadcasted_iota(jnp.int32, g.shape, 0)
    g1 = jnp.where(row == 0, carry[1:2], pltpu.roll(g, 1, axis=0))
    g2 = jnp.where(row == 0, carry[0:1], jnp.where(row == 1, carry[1:2], pltpu.roll(g, 2, axis=0)))
    cw = cw_ref[...]
    gc = cb_ref[...] + cw[0:1] * g2
    gc = gc + cw[1:2] * g1
    gc = gc + cw[2:3] * g
    h_ref[...] = (gc * jax.nn.sigmoid(gc) * u).astype(h_ref.dtype)
    last2 = g[tm - 2:tm, :]
    carry_ref[j, 0:2, :] = last2
    cs_ref[0] = last2


def _ffn_up(x, w_up, conv_w, conv_b, rows_per_seq, tm=1024, tn=256):
    m, d = x.shape
    f = w_up.shape[1] // 2
    tm = _tile(rows_per_seq, tm)
    tn = _tile(f, tn)
    nj = f // tn
    tps = rows_per_seq // tm
    h, tails = pl.pallas_call(
        functools.partial(_ffn_up_kernel, tiles_per_seq=tps),
        out_shape=(jax.ShapeDtypeStruct((m, f), BF16),
                   jax.ShapeDtypeStruct((m // tm, 2, f), F32)),
        grid=(m // tm, nj),
        in_specs=[pl.BlockSpec((tm, d), lambda i, j: (i, 0)),
                  pl.BlockSpec((d, tn), lambda i, j: (0, j)),
                  pl.BlockSpec((d, tn), lambda i, j: (0, nj + j)),
                  pl.BlockSpec((3, tn), lambda i, j: (0, j)),
                  pl.BlockSpec((1, tn), lambda i, j: (0, j))],
        out_specs=(pl.BlockSpec((tm, tn), lambda i, j: (i, j)),
                   pl.BlockSpec((1, 2, tn), lambda i, j: (i, 0, j))),
        scratch_shapes=[pltpu.VMEM((nj, SUBLANES, tn), F32)],
        compiler_params=_params("arbitrary", "arbitrary"),
        name="ffn_up",
    )(x, w_up, w_up, conv_w, conv_b.reshape(1, f))
    return h, tails[tps - 1::tps]


def _ffn_up_step_kernel(x_ref, wg_ref, wu_ref, cw_ref, cb_ref, p0_ref, p1_ref, h_ref, g_ref):
    x = x_ref[...]
    g = jnp.dot(x, wg_ref[...], preferred_element_type=F32)
    u = jnp.dot(x, wu_ref[...], preferred_element_type=F32)
    cw = cw_ref[...]
    gc = cb_ref[...] + cw[0:1] * p0_ref[...]
    gc = gc + cw[1:2] * p1_ref[...]
    gc = gc + cw[2:3] * g
    h_ref[...] = (gc * jax.nn.sigmoid(gc) * u).astype(h_ref.dtype)
    g_ref[...] = g


def _ffn_up_step(x, w_up, conv_w, conv_b, prev0, prev1, tn=256):
    m, d = x.shape
    f = w_up.shape[1] // 2
    tn = _tile(f, tn)
    nj = f // tn
    row = lambda j: (0, j)
    return pl.pallas_call(
        _ffn_up_step_kernel,
        out_shape=(jax.ShapeDtypeStruct((m, f), BF16), jax.ShapeDtypeStruct((m, f), F32)),
        grid=(nj,),
        in_specs=[pl.BlockSpec((m, d), lambda j: (0, 0)),
                  pl.BlockSpec((d, tn), row),
                  pl.BlockSpec((d, tn), lambda j: (0, nj + j)),
                  pl.BlockSpec((3, tn), row),
                  pl.BlockSpec((1, tn), row),
                  pl.BlockSpec((m, tn), row),
                  pl.BlockSpec((m, tn), row)],
        out_specs=(pl.BlockSpec((m, tn), row), pl.BlockSpec((m, tn), row)),
        compiler_params=_params("parallel"),
        name="ffn_up_step",
    )(x, w_up, w_up, conv_w, conv_b.reshape(1, f), prev0, prev1)


def _softplus(x):
    return jnp.maximum(x, 0.0) + jnp.log(1.0 + jnp.exp(-jnp.abs(x)))


def _rwkv_prep_kernel(p_ref, pp_ref, mu_ref, w0_ref, a0_ref, kk_ref, ka_ref, w2_ref, a2_ref, g2_ref, bd_ref,
                      r_ref, w_ref, k_ref, v_ref, kk_o_ref, b_ref, g_ref, *, c, lw, la):
    p = p_ref[...]
    xm = p + (pp_ref[...] - p) * mu_ref[...]
    r = xm[:, 0:c]
    k = xm[:, c:2 * c]
    v = xm[:, 2 * c:3 * c]
    o = 3 * c
    wl = xm[:, o:o + lw]
    al = xm[:, o + lw:o + lw + la]
    gl = xm[:, o + lw + la:]
    z = w0_ref[...] + jnp.dot(jnp.tanh(wl).astype(BF16), w2_ref[...], preferred_element_type=F32)
    w = -_softplus(-z) - 0.5
    a = jax.nn.sigmoid(a0_ref[...] + jnp.dot(al.astype(BF16), a2_ref[...], preferred_element_type=F32))
    g = jnp.dot(jax.nn.sigmoid(gl).astype(BF16), g2_ref[...], preferred_element_type=F32)
    kk = k * kk_ref[...]
    bd2 = bd_ref[...]
    for s in range(c // LANES):
        sl = slice(s * LANES, (s + 1) * LANES)
        kks = kk[:, sl]
        ss = _seg_dot(kks * kks, bd2)
        kkn = kks / jnp.maximum(jnp.sqrt(ss), 1e-12)
        kk_o_ref[:, sl] = kkn
        b_ref[:, sl] = kkn * a[:, sl]
    r_ref[...] = r
    w_ref[...] = jnp.exp(-jnp.exp(w))
    k_ref[...] = k * (1.0 + (a - 1.0) * ka_ref[...])
    v_ref[...] = v
    g_ref[...] = g


def _head_pair_indicator(n):
    i = jnp.arange(LANES)
    bd = (i[:, None] // n == i[None, :] // n).astype(BF16)
    return jnp.concatenate([bd, bd], axis=0)


def _rwkv_prep(cfg, p, p_prev, prm):
    m = p.shape[0]
    c = cfg.rwkv_width
    pw = p.shape[1]
    lw, la = _rup(cfg.lora_w, LANES), _rup(cfg.lora_a, LANES)
    lg = pw - 3 * c - lw - la
    tm = _tile(m, 128)
    full = lambda i: (0, 0)
    rowblk = lambda i: (i, 0)
    out = jax.ShapeDtypeStruct((m, c), F32)
    return pl.pallas_call(
        functools.partial(_rwkv_prep_kernel, c=c, lw=lw, la=la),
        out_shape=(out,) * 7,
        grid=(m // tm,),
        in_specs=[pl.BlockSpec((tm, pw), rowblk), pl.BlockSpec((tm, pw), rowblk),
                  pl.BlockSpec((1, pw), full), pl.BlockSpec((1, c), full), pl.BlockSpec((1, c), full),
                  pl.BlockSpec((1, c), full), pl.BlockSpec((1, c), full),
                  pl.BlockSpec((lw, c), full), pl.BlockSpec((la, c), full), pl.BlockSpec((lg, c), full),
                  pl.BlockSpec((2 * LANES, LANES), full)],
        out_specs=(pl.BlockSpec((tm, c), rowblk),) * 7,
        compiler_params=_params("parallel"),
        name="rwkv_prep",
    )(p, p_prev, prm["mu"], prm["w0"], prm["a0"], prm["kk"], prm["ka"], prm["w2"], prm["a2"], prm["g2"],
      _head_pair_indicator(cfg.rwkv_head_dim))


def _rwkv_scan_kernel(r_ref, w_ref, k_ref, v_ref, kk_ref, b_ref, s0_ref, bd_ref, eye_ref,
                      y_ref, sf_ref, st_ref, *, n_pairs, tc):
    t_blk = pl.program_id(2)

    @pl.when(t_blk == 0)
    def _():
        st_ref[...] = s0_ref[0]

    bd2 = bd_ref[...]
    eye = eye_ref[...]
    rows = min(tc, SUBLANES)

    def group(t0):
        rid = lax.broadcasted_iota(jnp.int32, (rows, LANES), 0)
        for p in range(n_pairs):
            sl = slice(p * LANES, (p + 1) * LANES)
            r, w, k, v, kk, b = (ref[0, pl.ds(t0, rows), sl] for ref in
                                 (r_ref, w_ref, k_ref, v_ref, kk_ref, b_ref))
            s = st_ref[p]
            y = jnp.zeros((rows, LANES), F32)
            for i in range(rows):
                z = _seg_dot(s * kk[i:i + 1], bd2)
                vb = _seg_dot(eye * v[i:i + 1], bd2)
                s = s * w[i:i + 1] - z * b[i:i + 1] + vb * k[i:i + 1]
                y2 = _seg_dot(s * r[i:i + 1], bd2)
                y = jnp.where(rid == i, jnp.sum(y2 * eye, axis=0, keepdims=True), y)
            st_ref[p] = s
            y_ref[0, pl.ds(t0, rows), sl] = y

    if tc <= SUBLANES:
        group(0)
    else:
        def body(g, carry):
            group(pl.multiple_of(g * SUBLANES, SUBLANES))
            return carry
        lax.fori_loop(0, tc // SUBLANES, body, 0)

    @pl.when(t_blk == pl.num_programs(2) - 1)
    def _():
        sf_ref[0] = st_ref[...]


def _rwkv_scan(cfg, r, w, k, v, kk, b, s0, n_seq, t_len):
    c = cfg.rwkv_width
    n = cfg.rwkv_head_dim
    n_pairs_total = c // LANES
    pg = min(8, n_pairs_total)
    tc = _tile(t_len, 256)
    i = jnp.arange(LANES)
    eye = (i[None, :] % n == jnp.arange(n)[:, None]).astype(F32)
    seq = pl.BlockSpec((1, tc, pg * LANES), lambda bi, g, t: (bi, t, g))
    st = pl.BlockSpec((1, pg, n, LANES), lambda bi, g, t: (bi, g, 0, 0))
    full = lambda bi, g, t: (0, 0)
    shp = lambda a: a.reshape(n_seq, t_len, c)
    y, sf = pl.pallas_call(
        functools.partial(_rwkv_scan_kernel, n_pairs=pg, tc=tc),
        out_shape=(jax.ShapeDtypeStruct((n_seq, t_len, c), F32),
                   jax.ShapeDtypeStruct((n_seq, n_pairs_total, n, LANES), F32)),
        grid=(n_seq, n_pairs_total // pg, t_len // tc),
        in_specs=[seq] * 6 + [st, pl.BlockSpec((2 * LANES, LANES), full), pl.BlockSpec((n, LANES), full)],
        out_specs=(seq, st),
        scratch_shapes=[pltpu.VMEM((pg, n, LANES), F32)],
        compiler_params=_params("parallel", "parallel", "arbitrary"),
        name="rwkv_scan",
    )(shp(r), shp(w), shp(k), shp(v), shp(kk), shp(b), s0, _head_pair_indicator(n), eye)
    return y.reshape(n_seq * t_len, c), sf


def _rwkv_post_kernel(y_ref, r_ref, k_ref, v_ref, g_ref, lnw_ref, lnb_ref, rk_ref, bd_ref, o_ref, *, n, eps):
    bd2 = bd_ref[...]
    for s in range(y_ref.shape[1] // LANES):
        sl = slice(s * LANES, (s + 1) * LANES)
        y = y_ref[:, sl]
        mean = _seg_dot(y, bd2) / n
        d = y - mean
        var = _seg_dot(d * d, bd2) / n
        yn = d * lax.rsqrt(var + eps) * lnw_ref[:, sl] + lnb_ref[:, sl]
        bonus = _seg_dot(r_ref[:, sl] * k_ref[:, sl] * rk_ref[:, sl], bd2)
        yn = yn + bonus * v_ref[:, sl]
        o_ref[:, sl] = (yn * g_ref[:, sl]).astype(o_ref.dtype)


def _rwkv_post(cfg, y, r, k, v, g, prm):
    m, c = y.shape
    tm = _tile(m, 256)
    rowblk = lambda i: (i, 0)
    full = lambda i: (0, 0)
    big = pl.BlockSpec((tm, c), rowblk)
    vec = pl.BlockSpec((1, c), full)
    return pl.pallas_call(
        functools.partial(_rwkv_post_kernel, n=float(cfg.rwkv_head_dim), eps=cfg.gn_eps),
        out_shape=jax.ShapeDtypeStruct((m, c), BF16),
        grid=(m // tm,),
        in_specs=[big] * 5 + [vec] * 3 + [pl.BlockSpec((2 * LANES, LANES), full)],
        out_specs=big,
        compiler_params=_params("parallel"),
        name="rwkv_post",
    )(y, r, k, v, g, prm["ln_w"], prm["ln_b"], prm["rk"], _head_pair_indicator(cfg.rwkv_head_dim))


def _diff_lambda(lp, lambda_init):
    s1 = jnp.sum(lp[0:1] * lp[1:2], axis=-1, keepdims=True)
    s2 = jnp.sum(lp[2:3] * lp[3:4], axis=-1, keepdims=True)
    return jnp.exp(s1) - jnp.exp(s2) + lambda_init


def _softmax_rows(s):
    m = jnp.max(s, axis=-1, keepdims=True)
    e = jnp.exp(s - m)
    return e / jnp.sum(e, axis=-1, keepdims=True)


def _diff_prompt_kernel(lp_ref, q_ref, k_ref, v_ref, g_ref, o_ref, *, tq, d, lambda_init, eps):
    lam = _diff_lambda(lp_ref[...], lambda_init)
    s_len = q_ref.shape[0]
    kb = k_ref[...].astype(BF16)
    vb = v_ref[...].astype(BF16)
    scale = d ** -0.5
    nt = (((1,), (1,)), ((), ()))
    for qi in range(s_len // tq):
        nk = (qi + 1) * tq
        q = q_ref[qi * tq:(qi + 1) * tq, :].astype(BF16)
        qpos = qi * tq + lax.broadcasted_iota(jnp.int32, (tq, nk), 0)
        kpos = lax.broadcasted_iota(jnp.int32, (tq, nk), 1)
        ok = kpos <= qpos
        ps = []
        for c in range(2):
            s = lax.dot_general(q[:, c * d:(c + 1) * d], kb[:nk, c * d:(c + 1) * d], nt,
                                preferred_element_type=F32) * scale
            ps.append(_softmax_rows(jnp.where(ok, s, NEG_BIG)))
        attn = ps[0] - lam * ps[1]
        o = jnp.dot(attn.astype(BF16), vb[:nk], preferred_element_type=F32)
        o = o * lax.rsqrt(jnp.mean(o * o, axis=-1, keepdims=True) + eps) * g_ref[...]
        o_ref[qi * tq:(qi + 1) * tq, :] = (o * (1.0 - lambda_init)).astype(o_ref.dtype)


def _diff_prompt(cfg, lp, q, k, v, subln):
    m, wdt = q.shape
    hw = 2 * cfg.diff_head_dim
    s_len = cfg.seq
    blk = pl.BlockSpec((s_len, hw), lambda b, h: (b, h))
    return pl.pallas_call(
        functools.partial(_diff_prompt_kernel, tq=_tile(s_len, 256), d=cfg.diff_head_dim,
                          lambda_init=cfg.lambda_init, eps=cfg.subln_eps),
        out_shape=jax.ShapeDtypeStruct((m, wdt), BF16),
        grid=(m // s_len, wdt // hw),
        in_specs=[pl.BlockSpec(lp.shape, lambda b, h: (0, 0)), blk, blk, blk,
                  pl.BlockSpec((1, hw), lambda b, h: (0, 0))],
        out_specs=blk,
        compiler_params=_params("parallel", "parallel"),
        name="diff_prompt",
    )(lp, q, k, v, subln.reshape(1, hw))


def _diff_scores_kernel(pt_ref, q_ref, k_ref, ind_ref, s_ref, *, scale):
    del pt_ref
    qbd = (ind_ref[...] * q_ref[0]).astype(BF16)
    nt = (((1,), (1,)), ((), ()))
    s_ref[0] = lax.dot_general(qbd, k_ref[0].astype(BF16), nt, preferred_element_type=F32) * scale


def _diff_weights_kernel(lp_ref, s_ref, q_ref, k_ref, ind_ref, w_ref, wn_ref, *, scale, lambda_init, h):
    lam = _diff_lambda(lp_ref[...], lambda_init)
    qbd = (ind_ref[...] * q_ref[0]).astype(BF16).astype(F32)
    sn = jnp.sum(qbd * k_ref[0].astype(BF16).astype(F32), axis=-1, keepdims=True) * scale
    s = s_ref[0]
    mx = jnp.maximum(jnp.max(s, axis=-1, keepdims=True), sn)
    e = jnp.exp(s - mx)
    en = jnp.exp(sn - mx)
    den = jnp.sum(e, axis=-1, keepdims=True) + en
    p = e / den
    pn = en / den
    w_ref[0] = p[0:h] - lam * p[h:2 * h]
    wn_ref[0] = pn[0:h] - lam * pn[h:2 * h]


def _diff_pv_kernel(pt_ref, w_ref, wn_ref, v_ref, vn_ref, ex_ref, subln_ref, o_ref, acc_ref, *,
                    hw, lambda_init, eps):
    del pt_ref
    pg = pl.program_id(1)

    @pl.when(pg == 0)
    def _():
        acc_ref[...] = wn_ref[0].astype(BF16).astype(F32) * vn_ref[0].astype(BF16).astype(F32)

    acc_ref[...] += jnp.dot(w_ref[0].astype(BF16), v_ref[0].astype(BF16), preferred_element_type=F32)

    @pl.when(pg == pl.num_programs(1) - 1)
    def _():
        o = acc_ref[...] * ex_ref[...]
        ms = jnp.sum(o * o, axis=-1, keepdims=True) / hw
        o = o * lax.rsqrt(ms + eps)
        o_ref[0] = jnp.sum(o, axis=0, keepdims=True) * subln_ref[...] * (1.0 - lambda_init)


def _diff_sample(cfg, lp, q, k, v, cache_k, cache_v, page_table, subln):
    db, wdt = q.shape
    h = cfg.diff_heads
    d = cfg.diff_head_dim
    hw = 2 * d
    ps = cfg.page_size
    n_pages = page_table.shape[1]
    n_past = n_pages * ps
    n_pool = cache_k.shape[0]
    ck = cache_k.reshape(n_pool, ps, wdt)
    cv = cache_v.reshape(n_pool, ps, wdt)
    scale = d ** -0.5
    lane = jnp.arange(wdt)
    ind = ((lane // d)[None, :] == jnp.concatenate([2 * jnp.arange(h), 2 * jnp.arange(h) + 1])[:, None]).astype(F32)
    expand = (jnp.arange(h)[:, None] == (lane // hw)[None, :]).astype(F32)
    q3, k3, v3 = q.reshape(db, 1, wdt), k.reshape(db, 1, wdt), v.reshape(db, 1, wdt)
    row3 = pl.BlockSpec((1, 1, wdt), lambda b, p, pt: (b, 0, 0))
    page3 = pl.BlockSpec((1, ps, wdt), lambda b, p, pt: (pt[b, p], 0, 0))
    s = pl.pallas_call(
        functools.partial(_diff_scores_kernel, scale=scale),
        out_shape=jax.ShapeDtypeStruct((db, 2 * h, n_past), F32),
        grid_spec=pltpu.PrefetchScalarGridSpec(
            num_scalar_prefetch=1, grid=(db, n_pages),
            in_specs=[row3, page3, pl.BlockSpec((2 * h, wdt), lambda b, p, pt: (0, 0))],
            out_specs=pl.BlockSpec((1, 2 * h, ps), lambda b, p, pt: (b, 0, p))),
        compiler_params=_params("parallel", "arbitrary"),
        name="diff_scores",
    )(page_table, q3, ck, ind)
    rowb = pl.BlockSpec((1, 1, wdt), lambda b: (b, 0, 0))
    w, wn = pl.pallas_call(
        functools.partial(_diff_weights_kernel, scale=scale, lambda_init=cfg.lambda_init, h=h),
        out_shape=(jax.ShapeDtypeStruct((db, h, n_past), F32), jax.ShapeDtypeStruct((db, h, 1), F32)),
        grid=(db,),
        in_specs=[pl.BlockSpec(lp.shape, lambda b: (0, 0)),
                  pl.BlockSpec((1, 2 * h, n_past), lambda b: (b, 0, 0)), rowb, rowb,
                  pl.BlockSpec((2 * h, wdt), lambda b: (0, 0))],
        out_specs=(pl.BlockSpec((1, h, n_past), lambda b: (b, 0, 0)),
                   pl.BlockSpec((1, h, 1), lambda b: (b, 0, 0))),
        compiler_params=_params("parallel"),
        name="diff_weights",
    )(lp, s, q3, k3, ind)
    o = pl.pallas_call(
        functools.partial(_diff_pv_kernel, hw=float(hw), lambda_init=cfg.lambda_init, eps=cfg.subln_eps),
        out_shape=jax.ShapeDtypeStruct((db, 1, wdt), F32),
        grid_spec=pltpu.PrefetchScalarGridSpec(
            num_scalar_prefetch=1, grid=(db, n_pages),
            in_specs=[pl.BlockSpec((1, h, ps), lambda b, p, pt: (b, 0, p)),
                      pl.BlockSpec((1, h, 1), lambda b, p, pt: (b, 0, 0)),
                      page3, row3,
                      pl.BlockSpec((h, wdt), lambda b, p, pt: (0, 0)),
                      pl.BlockSpec((1, wdt), lambda b, p, pt: (0, 0))],
            out_specs=row3,
            scratch_shapes=[pltpu.VMEM((h, wdt), F32)]),
        compiler_params=_params("parallel", "arbitrary"),
        name="diff_pv",
    )(page_table, w, wn, cv, v3, expand, jnp.tile(subln.reshape(1, hw), (1, h)))
    return o.reshape(db, wdt)


def _top_mask_lanes(gate, n_top):
    nb = gate.shape[1]
    lane = lax.broadcasted_iota(jnp.int32, gate.shape, 1).astype(F32)
    sel = jnp.zeros(gate.shape, F32)
    for _ in range(n_top):
        mx = jnp.max(gate, axis=1, keepdims=True)
        idx = jnp.min(jnp.where(gate == mx, lane, float(nb)), axis=1, keepdims=True)
        hit = (lane == idx) & (mx > 0.5 * NEG_BIG)
        sel = jnp.where(hit, 1.0, sel)
        gate = jnp.where(lane == idx, NEG_BIG, gate)
    return sel


def _moba_prompt_kernel(q_ref, k_ref, v_ref, ex_ref, o_ref, *, bs, n_top):
    s_len, d = q_ref.shape
    nb = s_len // bs
    kf = k_ref[...]
    kb = kf.astype(BF16)
    vb = v_ref[...].astype(BF16)
    ex = ex_ref[...]
    kmean = jnp.dot(ex.astype(F32), kf, precision=lax.Precision.HIGHEST, preferred_element_type=F32) / bs
    scale = d ** -0.5
    nt = (((1,), (1,)), ((), ()))
    for qi in range(nb):
        nk = (qi + 1) * bs
        qf = q_ref[qi * bs:(qi + 1) * bs, :]
        s = lax.dot_general(qf.astype(BF16), kb[:nk], nt, preferred_element_type=F32) * scale
        qpos = qi * bs + lax.broadcasted_iota(jnp.int32, (bs, nk), 0)
        kpos = lax.broadcasted_iota(jnp.int32, (bs, nk), 1)
        ok = (kpos <= qpos) & (kpos >= qi * bs)
        if qi > 0:
            gate = lax.dot_general(qf, kmean, nt, precision=lax.Precision.HIGHEST,
                                   preferred_element_type=F32)
            blk = lax.broadcasted_iota(jnp.int32, gate.shape, 1)
            sel = _top_mask_lanes(jnp.where(blk < qi, gate, NEG_BIG), n_top)
            selx = jnp.dot(sel.astype(BF16), ex[:, :nk], preferred_element_type=F32)
            ok = ok | (selx > 0.5)
        p = _softmax_rows(jnp.where(ok, s, NEG_BIG))
        o_ref[qi * bs:(qi + 1) * bs, :] = jnp.dot(p.astype(BF16), vb[:nk],
                                                  preferred_element_type=F32).astype(o_ref.dtype)


def _moba_prompt(cfg, q, k, v):
    m, wdt = q.shape
    d = cfg.moba_head_dim
    s_len = cfg.seq
    bs = cfg.moba_block
    nb = s_len // bs
    nbp = _rup(nb, 2 * SUBLANES)
    expand = (jnp.arange(nbp)[:, None] == (jnp.arange(s_len) // bs)[None, :]).astype(BF16)
    blk = pl.BlockSpec((s_len, d), lambda b, h: (b, h))
    return pl.pallas_call(
        functools.partial(_moba_prompt_kernel, bs=bs, n_top=min(cfg.moba_topk, nb)),
        out_shape=jax.ShapeDtypeStruct((m, wdt), BF16),
        grid=(m // s_len, wdt // d),
        in_specs=[blk, blk, blk, pl.BlockSpec((nbp, s_len), lambda b, h: (0, 0))],
        out_specs=blk,
        compiler_params=_params("parallel", "parallel"),
        name="moba_prompt",
    )(q, k, v, expand)


def _moba_block_sums_kernel(pt_ref, *refs, ppb):
    del pt_ref
    k_refs, o_ref = refs[:ppb], refs[ppb]
    blk = pl.program_id(1)
    tot = jnp.sum(k_refs[0][0], axis=0, keepdims=True)
    for r in k_refs[1:]:
        tot = tot + jnp.sum(r[0], axis=0, keepdims=True)
    o_ref[0, pl.ds(blk, 1), :] = tot


def _moba_select_kernel(ks_ref, q_ref, kn_ref, g_ref, sel_ref, *, n_past_blocks, own_block, bs, n_top):
    q = q_ref[0]
    rows = ks_ref.shape[1]
    ksum = ks_ref[0]
    rid = lax.broadcasted_iota(jnp.int32, (rows, 1), 0)
    ksum = ksum + jnp.where(rid == own_block, kn_ref[0], 0.0)
    kmean = ksum / bs
    hi, mid = _split_bf16(kmean * q)
    lo = (kmean * q - hi.astype(F32) - mid.astype(F32)).astype(BF16)
    g = g_ref[...]
    gate = (jnp.dot(hi, g, preferred_element_type=F32) + jnp.dot(mid, g, preferred_element_type=F32)
            + jnp.dot(lo, g, preferred_element_type=F32))
    rfull = lax.broadcasted_iota(jnp.int32, gate.shape, 0).astype(F32)
    gate = jnp.where((rfull < own_block) & (rfull < n_past_blocks), gate, NEG_BIG)
    out = jnp.zeros(sel_ref.shape[1:], F32)
    orow = lax.broadcasted_iota(jnp.int32, out.shape, 0)
    for j in range(n_top):
        mx = jnp.max(gate, axis=0, keepdims=True)
        idx = jnp.min(jnp.where(gate == mx, rfull, float(rows)), axis=0, keepdims=True)
        out = jnp.where(orow == j, idx, out)
        gate = jnp.where(rfull == idx, NEG_BIG, gate)
    sel_ref[0] = out.astype(jnp.int32)


def _moba_sample_attn_kernel(sel_ref, pt_ref, q_ref, kn_ref, vn_ref, *refs, n_pg, scale):
    del sel_ref, pt_ref
    k_refs, v_refs, o_ref = refs[:n_pg], refs[n_pg:2 * n_pg], refs[2 * n_pg]
    qb = jnp.broadcast_to(q_ref[0], (SUBLANES, q_ref.shape[2])).astype(BF16)
    nt = (((1,), (1,)), ((), ()))
    s = [lax.dot_general(qb, r[0].astype(BF16), nt, preferred_element_type=F32) * scale for r in k_refs]
    knb = jnp.broadcast_to(kn_ref[0], qb.shape).astype(BF16).astype(F32)
    sn = jnp.sum(qb.astype(F32) * knb, axis=-1, keepdims=True) * scale
    mx = sn
    for x in s:
        mx = jnp.maximum(mx, jnp.max(x, axis=-1, keepdims=True))
    e = [jnp.exp(x - mx) for x in s]
    en = jnp.exp(sn - mx)
    den = en
    for x in e:
        den = den + jnp.sum(x, axis=-1, keepdims=True)
    o = (en / den).astype(BF16).astype(F32) * vn_ref[0].astype(BF16).astype(F32)
    for x, r in zip(e, v_refs):
        o = o + jnp.dot((x / den).astype(BF16), r[0].astype(BF16), preferred_element_type=F32)
    o_ref[0] = o[0:1]


def _moba_sample(cfg, q, k, v, cache_k, cache_v, page_table):
    db, wdt = q.shape
    d = cfg.moba_head_dim
    nh = cfg.moba_heads
    ps, bs = cfg.page_size, cfg.moba_block
    ppb = bs // ps
    n_pages = page_table.shape[1]
    n_pool = cache_k.shape[0]
    assert n_pages % ppb == 0 and cfg.past_len == n_pages * ps
    nbp = n_pages // ppb
    own = cfg.past_len // bs
    n_top = cfg.moba_topk
    assert nbp >= n_top and own == nbp and nh <= LANES
    ck = cache_k.reshape(n_pool, ps, wdt)
    cv = cache_v.reshape(n_pool, ps, wdt)
    rows = _rup(nbp + 1, SUBLANES)
    q3, k3, v3 = q.reshape(db, 1, wdt), k.reshape(db, 1, wdt), v.reshape(db, 1, wdt)

    def page_spec(r):
        return pl.BlockSpec((1, ps, wdt), lambda b, n, pt: (pt[b, n * ppb + r], 0, 0))

    ksum = pl.pallas_call(
        functools.partial(_moba_block_sums_kernel, ppb=ppb),
        out_shape=jax.ShapeDtypeStruct((db, nbp, wdt), F32),
        grid_spec=pltpu.PrefetchScalarGridSpec(
            num_scalar_prefetch=1, grid=(db, nbp),
            in_specs=[page_spec(r) for r in range(ppb)],
            out_specs=pl.BlockSpec((1, nbp, wdt), lambda b, n, pt: (b, 0, 0))),
        compiler_params=_params("parallel", "arbitrary"),
        name="moba_block_sums",
    )(page_table, *([ck] * ppb))
    ksum = jnp.pad(ksum, ((0, 0), (0, rows - nbp), (0, 0)))
    head_ind = ((jnp.arange(wdt) // d)[:, None] == jnp.arange(LANES)[None, :]).astype(BF16)
    rowb = pl.BlockSpec((1, 1, wdt), lambda b: (b, 0, 0))
    sel = pl.pallas_call(
        functools.partial(_moba_select_kernel, n_past_blocks=nbp, own_block=own, bs=float(bs), n_top=n_top),
        out_shape=jax.ShapeDtypeStruct((db, SUBLANES, LANES), jnp.int32),
        grid=(db,),
        in_specs=[pl.BlockSpec((1, rows, wdt), lambda b: (b, 0, 0)), rowb, rowb,
                  pl.BlockSpec((wdt, LANES), lambda b: (0, 0))],
        out_specs=pl.BlockSpec((1, SUBLANES, LANES), lambda b: (b, 0, 0)),
        compiler_params=_params("parallel"),
        name="moba_select",
    )(ksum, q3, k3, head_ind)
    sel_flat = jnp.transpose(sel[:, :n_top, :nh], (0, 2, 1)).reshape(-1)
    pt_flat = page_table.reshape(-1)
    n_pg = n_top * ppb

    def sel_page(jp):
        j, r = divmod(jp, ppb)
        return pl.BlockSpec(
            (1, ps, d),
            lambda b, h, sl, pt: (pt[b * n_pages + sl[(b * nh + h) * n_top + j] * ppb + r], 0, h))

    headb = pl.BlockSpec((1, 1, d), lambda b, h, sl, pt: (b, 0, h))
    o = pl.pallas_call(
        functools.partial(_moba_sample_attn_kernel, n_pg=n_pg, scale=d ** -0.5),
        out_shape=jax.ShapeDtypeStruct((db, 1, wdt), F32),
        grid_spec=pltpu.PrefetchScalarGridSpec(
            num_scalar_prefetch=2, grid=(db, nh),
            in_specs=[headb, headb, headb] + [sel_page(jp) for jp in range(n_pg)] * 2,
            out_specs=headb),
        compiler_params=_params("parallel", "parallel"),
        name="moba_sample_attn",
    )(sel_flat, pt_flat, q3, k3, v3, *([ck] * n_pg), *([cv] * n_pg))
    return o.reshape(db, wdt)


def _rope_tables(cfg, pos, d):
    half = d // 2
    inv = cfg.rope_theta ** (-jnp.arange(half, dtype=F32) / half)
    ang = pos.astype(F32)[:, None] * inv[None, :]
    cos, sin = jnp.cos(ang), jnp.sin(ang)
    return jnp.concatenate([cos, cos], axis=-1), jnp.concatenate([-sin, sin], axis=-1)


def _rwkv_layout(cfg):
    c = cfg.rwkv_width
    segs, o = [], 0
    for wd in (c, c, c, cfg.lora_w, cfg.lora_a, cfg.lora_g):
        segs.append((o, wd, _rup(wd, LANES)))
        o += wd
    return segs, o


def _pad_cols(cfg, a):
    segs, _ = _rwkv_layout(cfg)
    parts = []
    for o, wd, pw in segs:
        x = a[..., o:o + wd]
        if pw != wd:
            x = jnp.pad(x, [(0, 0)] * (a.ndim - 1) + [(0, pw - wd)])
        parts.append(x)
    return jnp.concatenate(parts, axis=-1)


def _unpad_cols(cfg, a):
    segs, _ = _rwkv_layout(cfg)
    parts, o = [], 0
    for _, wd, pw in segs:
        parts.append(a[..., o:o + wd])
        o += pw
    return jnp.concatenate(parts, axis=-1)


def _pad_rows(a, rows):
    return jnp.pad(a, ((0, rows - a.shape[0]), (0, 0)))


def _state_to_pairs(cfg, s):
    b, h, n, _ = s.shape
    return jnp.transpose(s.reshape(b, h // 2, 2, n, n), (0, 1, 3, 2, 4)).reshape(b, h // 2, n, 2 * n)


def _pairs_to_state(cfg, s):
    b, hp, n, _ = s.shape
    return jnp.transpose(s.reshape(b, hp, n, 2, n), (0, 1, 3, 2, 4)).reshape(b, 2 * hp, n, n)


def _forward(cfg, x_prompt, x_sample, cache_k0, cache_v0, cache_k1, cache_v1, page_table,
             state_wkv0, state_shift0, state_conv,
             norm_mix, norm_ffn, norm_final, w_in0, w_out0,
             rwkv_mu, rwkv_w0, rwkv_w2, rwkv_a0, rwkv_a2, rwkv_g2, rwkv_kk, rwkv_ka, rwkv_rk,
             rwkv_ln_w, rwkv_ln_b, diff_lq1, diff_lk1, diff_lq2, diff_lk2, diff_subln,
             w_in1, w_out1, ffn_w_up, ffn_conv_w, ffn_conv_b, ffn_w_down):
    bsz, s_len, dm = x_prompt.shape
    db = x_sample.shape[0]
    c = cfg.rwkv_width
    dw = dm - c
    segs, rproj = _rwkv_layout(cfg)
    n_heads = c // cfg.rwkv_head_dim

    w_r = _pad_cols(cfg, w_in0[:, :rproj]).astype(BF16)
    w_q, w_k, w_v = (w_in0[:, rproj + i * dw:rproj + (i + 1) * dw].astype(BF16) for i in range(3))
    w_out0_a, w_out0_b = w_out0[:c].astype(BF16), w_out0[c:].astype(BF16)
    w_in1_b = w_in1.astype(BF16)
    w_out1_b = w_out1.astype(BF16)
    w_up_b = ffn_w_up.astype(BF16)
    w_down_b = ffn_w_down.astype(BF16)
    lw_p, la_p = _rup(cfg.lora_w, LANES), _rup(cfg.lora_a, LANES)
    row = lambda a: a.reshape(1, -1)
    prm = dict(mu=row(_pad_cols(cfg, rwkv_mu)), w0=row(rwkv_w0), a0=row(rwkv_a0), kk=row(rwkv_kk),
               ka=row(rwkv_ka), w2=_pad_rows(rwkv_w2, lw_p).astype(BF16),
               a2=_pad_rows(rwkv_a2, la_p).astype(BF16),
               g2=_pad_rows(rwkv_g2, _rup(cfg.lora_g, LANES)).astype(BF16),
               ln_w=row(rwkv_ln_w), ln_b=row(rwkv_ln_b), rk=row(rwkv_rk))
    lp = jnp.stack([diff_lq1, diff_lk1, diff_lq2, diff_lk2])
    rope_p = _rope_tables(cfg, jnp.arange(s_len, dtype=jnp.int32), cfg.diff_head_dim)
    rope_s = _rope_tables(cfg, jnp.full((db,), cfg.past_len, jnp.int32), cfg.diff_head_dim)

    xp = x_prompt.reshape(bsz * s_len, dm)
    xs = x_sample.reshape(db, dm)
    out = {}
    conv_p, conv_s = [], []
    for layer in range(2):
        hp = _rmsnorm(xp, norm_mix[layer], cfg.norm_eps, BF16)
        hs = _rmsnorm(xs, norm_mix[layer], cfg.norm_eps, BF16)
        if layer == 0:
            mix = {}
            for tag, h, n_seq, t_len, rope in (("p", hp, bsz, s_len, rope_p), ("s", hs, db, 1, rope_s)):
                rows = n_seq * t_len
                p = _matmul(h, w_r)
                if tag == "p":
                    p3 = p.reshape(n_seq, t_len, -1)
                    p_prev = jnp.concatenate([jnp.zeros_like(p3[:, :1]), p3[:, :-1]], axis=1).reshape(rows, -1)
                    s0 = jnp.zeros((n_seq, n_heads // 2, cfg.rwkv_head_dim, LANES), F32)
                    shift = p3[:, -1]
                else:
                    p_prev = _pad_cols(cfg, state_shift0)
                    s0 = _state_to_pairs(cfg, state_wkv0)
                    shift = p
                r, w, k, v, kk, b, g = _rwkv_prep(cfg, p, p_prev, prm)
                y, sf = _rwkv_scan(cfg, r, w, k, v, kk, b, s0, n_seq, t_len)
                a_out = _rwkv_post(cfg, y, r, k, v, g, prm)
                q0 = _matmul(h, w_q, rope=rope, rows_per_seq=t_len if tag == "p" else db)
                k0 = _matmul(h, w_k, rope=rope, rows_per_seq=t_len if tag == "p" else db)
                v0 = _matmul(h, w_v)
                if tag == "p":
                    b_out = _diff_prompt(cfg, lp, q0, k0, v0, diff_subln)
                else:
                    b_out = _diff_sample(cfg, lp, q0, k0, v0, cache_k0, cache_v0, page_table,
                                         diff_subln).astype(BF16)
                mix[tag] = (a_out, b_out)
                out["k0" + tag] = k0.reshape(n_seq, t_len, 2 * cfg.diff_heads, cfg.diff_head_dim)
                out["v0" + tag] = v0.reshape(n_seq, t_len, cfg.diff_heads, 2 * cfg.diff_head_dim)
                out["wkv" + tag] = _pairs_to_state(cfg, sf)
                out["shift" + tag] = _unpad_cols(cfg, shift)
            xp = _matmul(mix["p"][0], w_out0_a, x2=mix["p"][1], w2=w_out0_b, res=xp)
            xs = _matmul(mix["s"][0], w_out0_a, x2=mix["s"][1], w2=w_out0_b, res=xs)
        else:
            qkv = []
            for tag, h, n_seq, t_len, rope in (("p", hp, bsz, s_len, rope_p), ("s", hs, db, 1, rope_s)):
                rps = t_len if tag == "p" else db
                q1 = _matmul(h, w_in1_b[:, :dm], rope=rope, rows_per_seq=rps)
                k1 = _matmul(h, w_in1_b[:, dm:2 * dm], rope=rope, rows_per_seq=rps)
                v1 = _matmul(h, w_in1_b[:, 2 * dm:])
                out["k1" + tag] = k1.reshape(n_seq, t_len, cfg.moba_heads, cfg.moba_head_dim)
                out["v1" + tag] = v1.reshape(n_seq, t_len, cfg.moba_heads, cfg.moba_head_dim)
                qkv.append((q1, k1, v1))
            mp = _moba_prompt(cfg, *qkv[0])
            ms = _moba_sample(cfg, *qkv[1], cache_k1, cache_v1, page_table).astype(BF16)
            xp = _matmul(mp, w_out1_b, res=xp)
            xs = _matmul(ms, w_out1_b, res=xs)
        hp = _rmsnorm(xp, norm_ffn[layer], cfg.norm_eps, BF16)
        hs = _rmsnorm(xs, norm_ffn[layer], cfg.norm_eps, BF16)
        fp, cp = _ffn_up(hp, w_up_b[layer], ffn_conv_w[layer], ffn_conv_b[layer], s_len)
        prev = state_conv[layer]
        fs, gs = _ffn_up_step(hs, w_up_b[layer], ffn_conv_w[layer], ffn_conv_b[layer], prev[:, 0], prev[:, 1])
        xp = _matmul(fp, w_down_b[layer], res=xp, tm=512, tn=256)
        xs = _matmul(fs, w_down_b[layer], res=xs, tm=512, tn=256)
        conv_p.append(cp)
        conv_s.append(jnp.stack([prev[:, 1], gs], axis=1))
    y_prompt = _rmsnorm(xp, norm_final, cfg.norm_eps, F32).reshape(bsz, s_len, dm)
    y_sample = _rmsnorm(xs, norm_final, cfg.norm_eps, F32).reshape(db, 1, dm)
    return (y_prompt, y_sample, out["k0p"], out["v0p"], out["k0s"], out["v0s"],
            out["wkvp"], out["wkvs"], out["shiftp"], out["shifts"],
            out["k1p"], out["v1p"], out["k1s"], out["v1s"], jnp.stack(conv_p), jnp.stack(conv_s))


def kernel(x_prompt, x_sample, cache_k0, cache_v0, cache_k1, cache_v1, page_table, state_wkv0, state_shift0, state_conv, norm_mix, norm_ffn, norm_final, w_in0, w_out0, rwkv_mu, rwkv_w0, rwkv_w2, rwkv_a0, rwkv_a2, rwkv_g2, rwkv_kk, rwkv_ka, rwkv_rk, rwkv_ln_w, rwkv_ln_b, diff_lq1, diff_lk1, diff_lq2, diff_lk2, diff_subln, w_in1, w_out1, ffn_w_up, ffn_conv_w, ffn_conv_b, ffn_w_down):
    return _forward(_CFG, x_prompt, x_sample, cache_k0, cache_v0, cache_k1, cache_v1, page_table,
                    state_wkv0, state_shift0, state_conv, norm_mix, norm_ffn, norm_final, w_in0, w_out0,
                    rwkv_mu, rwkv_w0, rwkv_w2, rwkv_a0, rwkv_a2, rwkv_g2, rwkv_kk, rwkv_ka, rwkv_rk,
                    rwkv_ln_w, rwkv_ln_b, diff_lq1, diff_lk1, diff_lq2, diff_lk2, diff_subln,
                    w_in1, w_out1, ffn_w_up, ffn_conv_w, ffn_conv_b, ffn_w_down)
```

```python
import functools
import math
from typing import NamedTuple

import jax
import jax.numpy as jnp
from jax import lax
from jax.experimental import pallas as pl
from jax.experimental.pallas import tpu as pltpu

F32 = jnp.float32
BF16 = jnp.bfloat16
LANES = 128
SUBLANES = 8
NEG_BIG = -1e30
VMEM_LIMIT = 56 * 1024 * 1024


class Cfg(NamedTuple):
    d_model: int
    batch: int
    seq: int
    dec_batch: int
    past_len: int
    page_size: int
    rwkv_width: int
    rwkv_head_dim: int
    lora_w: int
    lora_a: int
    lora_g: int
    diff_heads: int
    diff_head_dim: int
    moba_heads: int
    moba_head_dim: int
    moba_block: int
    moba_topk: int
    d_ff: int
    rope_theta: float
    norm_eps: float
    subln_eps: float
    gn_eps: float
    lambda_init: float


def _make_cfg(d_model=4096, batch=4, seq=2048, dec_batch=8, past_len=16384, page_size=128, d_ff=11008,
              diff_heads=8, moba_block=256):
    rw = d_model // 2
    lw = max(32, int(round(1.8 * rw ** 0.5 / 32)) * 32)
    lg = max(32, int(round(0.6 * rw ** 0.8 / 32)) * 32)
    return Cfg(d_model=d_model, batch=batch, seq=seq, dec_batch=dec_batch, past_len=past_len,
               page_size=page_size, rwkv_width=rw, rwkv_head_dim=64, lora_w=lw, lora_a=lw, lora_g=lg,
               diff_heads=diff_heads, diff_head_dim=(d_model - rw) // (2 * diff_heads),
               moba_heads=d_model // 128, moba_head_dim=128, moba_block=moba_block, moba_topk=3,
               d_ff=d_ff, rope_theta=10000.0, norm_eps=1e-6, subln_eps=1e-5, gn_eps=64e-5,
               lambda_init=0.8 - 0.6 * math.exp(-0.3 * 0))


_CFG = _make_cfg()


def _rup(n, m):
    return -(-n // m) * m


def _tile(n, pref):
    if n <= pref:
        return n
    t = pref
    while n % t:
        t -= SUBLANES
    return t


def _params(*sem):
    return pltpu.CompilerParams(dimension_semantics=sem, vmem_limit_bytes=VMEM_LIMIT)


def _split_bf16(x):
    hi = x.astype(BF16)
    lo = (x - hi.astype(F32)).astype(BF16)
    return hi, lo


def _seg_dot(x, ind2):
    hi, lo = _split_bf16(x)
    return jnp.dot(jnp.concatenate([hi, lo], axis=1), ind2, preferred_element_type=F32)


def _rms_kernel(x_ref, g_ref, o_ref, *, eps):
    x = x_ref[...]
    ms = jnp.mean(x * x, axis=-1, keepdims=True)
    o_ref[...] = (x * lax.rsqrt(ms + eps) * g_ref[...]).astype(o_ref.dtype)


def _rmsnorm(x, g, eps, out_dtype):
    m, d = x.shape
    tm = _tile(m, 256)
    return pl.pallas_call(
        functools.partial(_rms_kernel, eps=eps),
        out_shape=jax.ShapeDtypeStruct((m, d), out_dtype),
        grid=(m // tm,),
        in_specs=[pl.BlockSpec((tm, d), lambda i: (i, 0)), pl.BlockSpec((1, d), lambda i: (0, 0))],
        out_specs=pl.BlockSpec((tm, d), lambda i: (i, 0)),
        compiler_params=_params("parallel"),
        name="rmsnorm",
    )(x, g.reshape(1, d))


def _mm_kernel(*refs, has_res, has_rope, two_lhs):
    it = iter(refs)
    x_ref = next(it)
    x2_ref = next(it) if two_lhs else None
    w_ref = next(it)
    w2_ref = next(it) if two_lhs else None
    res_ref = next(it) if has_res else None
    cos_ref = next(it) if has_rope else None
    sin_ref = next(it) if has_rope else None
    o_ref = next(it)
    acc = jnp.dot(x_ref[...], w_ref[...], preferred_element_type=F32)
    if two_lhs:
        acc = acc + jnp.dot(x2_ref[...], w2_ref[...], preferred_element_type=F32)
    if has_rope:
        cos = cos_ref[...]
        sin = sin_ref[...]
        for s in range(acc.shape[1] // LANES):
            a = acc[:, s * LANES:(s + 1) * LANES]
            o_ref[:, s * LANES:(s + 1) * LANES] = (
                a * cos + pltpu.roll(a, LANES // 2, axis=1) * sin).astype(o_ref.dtype)
    else:
        if has_res:
            acc = acc + res_ref[...]
        o_ref[...] = acc.astype(o_ref.dtype)


def _matmul(x, w, *, cols=None, x2=None, w2=None, w2_row0=0, res=None, rope=None, rows_per_seq=None,
            out_dtype=F32, tm=1024, tn=512):
    m, k = x.shape
    col0, n = cols if cols is not None else (0, w.shape[1])
    tm = _tile(m if rope is None else rows_per_seq, tm)
    tn = _tile(n, tn)
    assert col0 % tn == 0
    jb = col0 // tn
    two = x2 is not None
    in_specs = [pl.BlockSpec((tm, k), lambda i, j: (i, 0))]
    args = [x]
    if two:
        in_specs.append(pl.BlockSpec((tm, x2.shape[1]), lambda i, j: (i, 0)))
        args.append(x2)
    in_specs.append(pl.BlockSpec((k, tn), lambda i, j: (0, jb + j)))
    args.append(w)
    if two:
        k2 = x2.shape[1]
        assert w2_row0 % k2 == 0
        in_specs.append(pl.BlockSpec((k2, tn), lambda i, j: (w2_row0 // k2, jb + j)))
        args.append(w2)
    if res is not None:
        in_specs.append(pl.BlockSpec((tm, tn), lambda i, j: (i, j)))
        args.append(res)
    if rope is not None:
        tps = rows_per_seq // tm
        for t in rope:
            in_specs.append(pl.BlockSpec((tm, LANES), lambda i, j: (i % tps, 0)))
            args.append(t)
    return pl.pallas_call(
        functools.partial(_mm_kernel, has_res=res is not None, has_rope=rope is not None, two_lhs=two),
        out_shape=jax.ShapeDtypeStruct((m, n), out_dtype),
        grid=(m // tm, n // tn),
        in_specs=in_specs,
        out_specs=pl.BlockSpec((tm, tn), lambda i, j: (i, j)),
        compiler_params=_params("parallel", "parallel"),
        name="matmul",
    )(*args)


def _ffn_up_kernel(x_ref, wg_ref, wu_ref, cw_ref, cb_ref, h_ref, cs_ref, carry_ref, *, tiles_per_seq):
    i = pl.program_id(0)
    j = pl.program_id(1)
    x = x_ref[...]
    g = jnp.dot(x, wg_ref[...], preferred_element_type=F32)
    u = jnp.dot(x, wu_ref[...], preferred_element_type=F32)
    tm = g.shape[0]

    @pl.when(i % tiles_per_seq == 0)
    def _():
        carry_ref[j] = jnp.zeros(carry_ref.shape[1:], F32)

    carry = carry_ref[j]
    row = lax.broadcasted_iota(jnp.int32, g.shape, 0)
    g1 = jnp.where(row == 0, carry[1:2], pltpu.roll(g, 1, axis=0))
    g2 = jnp.where(row == 0, carry[0:1], jnp.where(row == 1, carry[1:2], pltpu.roll(g, 2, axis=0)))
    cw = cw_ref[...]
    gc = cb_ref[...] + cw[0:1] * g2
    gc = gc + cw[1:2] * g1
    gc = gc + cw[2:3] * g
    h_ref[...] = (gc * jax.nn.sigmoid(gc) * u).astype(h_ref.dtype)
    last2 = g[tm - 2:tm, :]
    carry_ref[j, 0:2, :] = last2
    cs_ref[0] = last2


def _ffn_up(x, w_up, conv_w, conv_b, rows_per_seq, tm=1024, tn=256):
    m, d = x.shape
    f = w_up.shape[1] // 2
    tm = _tile(rows_per_seq, tm)
    tn = _tile(f, tn)
    nj = f // tn
    tps = rows_per_seq // tm
    h, tails = pl.pallas_call(
        functools.partial(_ffn_up_kernel, tiles_per_seq=tps),
        out_shape=(jax.ShapeDtypeStruct((m, f), BF16),
                   jax.ShapeDtypeStruct((m // tm, 2, f), F32)),
        grid=(m // tm, nj),
        in_specs=[pl.BlockSpec((tm, d), lambda i, j: (i, 0)),
                  pl.BlockSpec((d, tn), lambda i, j: (0, j)),
                  pl.BlockSpec((d, tn), lambda i, j: (0, nj + j)),
                  pl.BlockSpec((3, tn), lambda i, j: (0, j)),
                  pl.BlockSpec((1, tn), lambda i, j: (0, j))],
        out_specs=(pl.BlockSpec((tm, tn), lambda i, j: (i, j)),
                   pl.BlockSpec((1, 2, tn), lambda i, j: (i, 0, j))),
        scratch_shapes=[pltpu.VMEM((nj, SUBLANES, tn), F32)],
        compiler_params=_params("arbitrary", "arbitrary"),
        name="ffn_up",
    )(x, w_up, w_up, conv_w, conv_b.reshape(1, f))
    return h, tails[tps - 1::tps]


def _ffn_up_step_kernel(x_ref, wg_ref, wu_ref, cw_ref, cb_ref, p0_ref, p1_ref, h_ref, g_ref):
    x = x_ref[...]
    g = jnp.dot(x, wg_ref[...], preferred_element_type=F32)
    u = jnp.dot(x, wu_ref[...], preferred_element_type=F32)
    cw = cw_ref[...]
    gc = cb_ref[...] + cw[0:1] * p0_ref[...]
    gc = gc + cw[1:2] * p1_ref[...]
    gc = gc + cw[2:3] * g
    h_ref[...] = (gc * jax.nn.sigmoid(gc) * u).astype(h_ref.dtype)
    g_ref[...] = g


def _ffn_up_step(x, w_up, conv_w, conv_b, prev0, prev1, tn=256):
    m, d = x.shape
    f = w_up.shape[1] // 2
    tn = _tile(f, tn)
    nj = f // tn
    row = lambda j: (0, j)
    return pl.pallas_call(
        _ffn_up_step_kernel,
        out_shape=(jax.ShapeDtypeStruct((m, f), BF16), jax.ShapeDtypeStruct((m, f), F32)),
        grid=(nj,),
        in_specs=[pl.BlockSpec((m, d), lambda j: (0, 0)),
                  pl.BlockSpec((d, tn), row),
                  pl.BlockSpec((d, tn), lambda j: (0, nj + j)),
                  pl.BlockSpec((3, tn), row),
                  pl.BlockSpec((1, tn), row),
                  pl.BlockSpec((m, tn), row),
                  pl.BlockSpec((m, tn), row)],
        out_specs=(pl.BlockSpec((m, tn), row), pl.BlockSpec((m, tn), row)),
        compiler_params=_params("parallel"),
        name="ffn_up_step",
    )(x, w_up, w_up, conv_w, conv_b.reshape(1, f), prev0, prev1)


def _softplus(x):
    return jnp.maximum(x, 0.0) + jnp.log(1.0 + jnp.exp(-jnp.abs(x)))


def _rwkv_prep_kernel(p_ref, pp_ref, mu_ref, w0_ref, a0_ref, kk_ref, ka_ref, w2_ref, a2_ref, g2_ref, bd_ref,
                      r_ref, w_ref, k_ref, v_ref, kk_o_ref, b_ref, g_ref, *, c, lw, la):
    p = p_ref[...]
    xm = p + (pp_ref[...] - p) * mu_ref[...]
    r = xm[:, 0:c]
    k = xm[:, c:2 * c]
    v = xm[:, 2 * c:3 * c]
    o = 3 * c
    wl = xm[:, o:o + lw]
    al = xm[:, o + lw:o + lw + la]
    gl = xm[:, o + lw + la:]
    z = w0_ref[...] + jnp.dot(jnp.tanh(wl).astype(BF16), w2_ref[...], preferred_element_type=F32)
    w = -_softplus(-z) - 0.5
    a = jax.nn.sigmoid(a0_ref[...] + jnp.dot(al.astype(BF16), a2_ref[...], preferred_element_type=F32))
    g = jnp.dot(jax.nn.sigmoid(gl).astype(BF16), g2_ref[...], preferred_element_type=F32)
    kk = k * kk_ref[...]
    decay = jnp.exp(-jnp.exp(w))
    k2 = k * (1.0 + (a - 1.0) * ka_ref[...])
    bd2 = bd_ref[...]
    for s in range(c // LANES):
        sl = slice(s * LANES, (s + 1) * LANES)
        kks = kk[:, sl]
        ss = _seg_dot(kks * kks, bd2)
        kkn = kks / jnp.maximum(jnp.sqrt(ss), 1e-12)
        kk_o_ref[s] = kkn
        b_ref[s] = kkn * a[:, sl]
        r_ref[s] = r[:, sl]
        w_ref[s] = decay[:, sl]
        k_ref[s] = k2[:, sl]
        v_ref[s] = v[:, sl]
        g_ref[s] = g[:, sl]


def _head_pair_indicator(n):
    i = jnp.arange(LANES)
    bd = (i[:, None] // n == i[None, :] // n).astype(BF16)
    return jnp.concatenate([bd, bd], axis=0)


def _rwkv_prep(cfg, p, p_prev, prm):
    m = p.shape[0]
    c = cfg.rwkv_width
    pw = p.shape[1]
    lw, la = _rup(cfg.lora_w, LANES), _rup(cfg.lora_a, LANES)
    lg = pw - 3 * c - lw - la
    tm = _tile(m, 128)
    full = lambda i: (0, 0)
    rowblk = lambda i: (i, 0)
    out = jax.ShapeDtypeStruct((c // LANES, m, LANES), F32)
    return pl.pallas_call(
        functools.partial(_rwkv_prep_kernel, c=c, lw=lw, la=la),
        out_shape=(out,) * 7,
        grid=(m // tm,),
        in_specs=[pl.BlockSpec((tm, pw), rowblk), pl.BlockSpec((tm, pw), rowblk),
                  pl.BlockSpec((1, pw), full), pl.BlockSpec((1, c), full), pl.BlockSpec((1, c), full),
                  pl.BlockSpec((1, c), full), pl.BlockSpec((1, c), full),
                  pl.BlockSpec((lw, c), full), pl.BlockSpec((la, c), full), pl.BlockSpec((lg, c), full),
                  pl.BlockSpec((2 * LANES, LANES), full)],
        out_specs=(pl.BlockSpec((c // LANES, tm, LANES), lambda i: (0, i, 0)),) * 7,
        compiler_params=_params("parallel"),
        name="rwkv_prep",
    )(p, p_prev, prm["mu"], prm["w0"], prm["a0"], prm["kk"], prm["ka"], prm["w2"], prm["a2"], prm["g2"],
      _head_pair_indicator(cfg.rwkv_head_dim))


def _rwkv_scan_kernel(r_ref, w_ref, k_ref, v_ref, kk_ref, b_ref, s0_ref, bd_ref,
                      y_ref, sf_ref, ysc_ref, vt_ref, *st_refs, n_pairs, tc):
    t_blk = pl.program_id(2)

    @pl.when(t_blk == 0)
    def _():
        for p in range(n_pairs):
            st_refs[p][...] = s0_ref[0, p]

    rows = min(tc, SUBLANES)
    n = st_refs[0].shape[0]

    nt = (((1,), (1,)), ((), ()))
    row8 = lax.broadcasted_iota(jnp.int32, (SUBLANES, LANES), 0)
    lane8 = lax.broadcasted_iota(jnp.int32, (SUBLANES, LANES), 1)
    rmask = ((lane8 // n == row8 % 2) & (row8 < 4)).astype(F32)
    lane_n = lax.broadcasted_iota(jnp.int32, (n, LANES), 1)

    def group(t0):
        vs = [v_ref[p, pl.ds(t0, rows), :] for p in range(n_pairs)]
        pad = LANES - n_pairs * rows
        vt_ref[...] = jnp.transpose(
            jnp.concatenate(vs + ([jnp.zeros((pad, LANES), F32)] if pad else []), axis=0))
        for i in range(rows):
            for p in range(n_pairs):
                row = lambda ref: ref[p, pl.ds(t0 + i, 1), :]
                s = st_refs[p][...]
                z = _seg_dot(s * row(kk_ref), bd_ref[...])
                c = p * rows + i
                vb = jnp.where(lane_n < n, jnp.broadcast_to(vt_ref[0:n, c:c + 1], (n, LANES)),
                               jnp.broadcast_to(vt_ref[n:2 * n, c:c + 1], (n, LANES)))
                s = s * row(w_ref) - z * row(b_ref) + vb * row(k_ref)
                st_refs[p][...] = s
                r = row(r_ref)
                r_hi = r.astype(BF16).astype(F32)
                rm = (jnp.where(row8 < 2, r_hi, r - r_hi) * rmask).astype(BF16)
                y8 = lax.dot_general(rm, s.astype(BF16), nt, preferred_element_type=F32)
                y8 = y8[0:2] + y8[2:4]
                for hh in range(2):
                    ysc_ref[p, hh * rows + i:hh * rows + i + 1, :] = y8[hh:hh + 1]
        for p in range(n_pairs):
            y_ref[p, pl.ds(t0, rows), :] = jnp.concatenate(
                [ysc_ref[p, 0:rows, :], ysc_ref[p, rows:2 * rows, :]], axis=1)

    if tc <= SUBLANES:
        group(0)
    else:
        def body(g, carry):
            group(pl.multiple_of(g * SUBLANES, SUBLANES))
            return carry
        lax.fori_loop(0, tc // SUBLANES, body, 0)

    @pl.when(t_blk == pl.num_programs(2) - 1)
    def _():
        for p in range(n_pairs):
            sf_ref[0, p] = st_refs[p][...]


def _rwkv_scan(cfg, r, w, k, v, kk, b, s0, n_seq, t_len):
    c = cfg.rwkv_width
    n = cfg.rwkv_head_dim
    n_pairs_total = c // LANES
    pg = min(16, n_pairs_total)
    tc = _tile(t_len, 128)
    assert tc <= SUBLANES or tc % SUBLANES == 0
    bd2 = _head_pair_indicator(n)
    seq = pl.BlockSpec((pg, None, tc, LANES), lambda bi, g, t: (g, bi, t, 0))
    shp = lambda a: a.reshape(n_pairs_total, n_seq, t_len, LANES)
    st = pl.BlockSpec((1, pg, n, LANES), lambda bi, g, t: (bi, g, 0, 0))
    full = lambda bi, g, t: (0, 0)
    y, sf = pl.pallas_call(
        functools.partial(_rwkv_scan_kernel, n_pairs=pg, tc=tc),
        out_shape=(jax.ShapeDtypeStruct((n_pairs_total, n_seq, t_len, LANES), F32),
                   jax.ShapeDtypeStruct((n_seq, n_pairs_total, n, LANES), F32)),
        grid=(n_seq, n_pairs_total // pg, t_len // tc),
        in_specs=[seq] * 6 + [st, pl.BlockSpec((2 * LANES, LANES), full)],
        out_specs=(seq, st),
        scratch_shapes=[pltpu.VMEM((pg, 2 * SUBLANES, n), F32), pltpu.VMEM((LANES, LANES), F32)]
        + [pltpu.VMEM((n, LANES), F32)] * pg,
        compiler_params=_params("parallel", "parallel", "arbitrary"),
        name="rwkv_scan",
    )(shp(r), shp(w), shp(k), shp(v), shp(kk), shp(b), s0, bd2)
    return y.reshape(n_pairs_total, n_seq * t_len, LANES), sf


def _rwkv_post_kernel(y_ref, r_ref, k_ref, v_ref, g_ref, lnw_ref, lnb_ref, rk_ref, bd_ref, o_ref, *, n, eps):
    bd2 = bd_ref[...]
    for s in range(y_ref.shape[0]):
        sl = slice(s * LANES, (s + 1) * LANES)
        y = y_ref[s]
        mean = _seg_dot(y, bd2) / n
        d = y - mean
        var = _seg_dot(d * d, bd2) / n
        yn = d * lax.rsqrt(var + eps) * lnw_ref[:, sl] + lnb_ref[:, sl]
        bonus = _seg_dot(r_ref[s] * k_ref[s] * rk_ref[:, sl], bd2)
        yn = yn + bonus * v_ref[s]
        o_ref[:, sl] = (yn * g_ref[s]).astype(o_ref.dtype)


def _rwkv_post(cfg, y, r, k, v, g, prm):
    npair, m, _ = y.shape
    c = npair * LANES
    tm = _tile(m, 256)
    full = lambda i: (0, 0)
    big = pl.BlockSpec((npair, tm, LANES), lambda i: (0, i, 0))
    vec = pl.BlockSpec((1, c), full)
    return pl.pallas_call(
        functools.partial(_rwkv_post_kernel, n=float(cfg.rwkv_head_dim), eps=cfg.gn_eps),
        out_shape=jax.ShapeDtypeStruct((m, c), BF16),
        grid=(m // tm,),
        in_specs=[big] * 5 + [vec] * 3 + [pl.BlockSpec((2 * LANES, LANES), full)],
        out_specs=pl.BlockSpec((tm, c), lambda i: (i, 0)),
        compiler_params=_params("parallel"),
        name="rwkv_post",
    )(y, r, k, v, g, prm["ln_w"], prm["ln_b"], prm["rk"], _head_pair_indicator(cfg.rwkv_head_dim))


def _diff_lambda(lp, lambda_init):
    s1 = jnp.sum(lp[0:1] * lp[1:2], axis=-1, keepdims=True)
    s2 = jnp.sum(lp[2:3] * lp[3:4], axis=-1, keepdims=True)
    return jnp.exp(s1) - jnp.exp(s2) + lambda_init


def _softmax_rows(s):
    m = jnp.max(s, axis=-1, keepdims=True)
    e = jnp.exp(s - m)
    return e / jnp.sum(e, axis=-1, keepdims=True)


def _diff_prompt_kernel(lp_ref, q_ref, k_ref, v_ref, g_ref, o_ref, *, tq, d, lambda_init, eps):
    lam = _diff_lambda(lp_ref[...], lambda_init)
    s_len = q_ref.shape[0]
    kb = k_ref[...].astype(BF16)
    vb = v_ref[...].astype(BF16)
    scale = d ** -0.5
    nt = (((1,), (1,)), ((), ()))
    for qi in range(s_len // tq):
        nk = (qi + 1) * tq
        q = q_ref[qi * tq:(qi + 1) * tq, :].astype(BF16)
        qpos = qi * tq + lax.broadcasted_iota(jnp.int32, (tq, nk), 0)
        kpos = lax.broadcasted_iota(jnp.int32, (tq, nk), 1)
        ok = kpos <= qpos
        ps = []
        for c in range(2):
            s = lax.dot_general(q[:, c * d:(c + 1) * d], kb[:nk, c * d:(c + 1) * d], nt,
                                preferred_element_type=F32) * scale
            ps.append(_softmax_rows(jnp.where(ok, s, NEG_BIG)))
        attn = ps[0] - lam * ps[1]
        o = jnp.dot(attn.astype(BF16), vb[:nk], preferred_element_type=F32)
        o = o * lax.rsqrt(jnp.mean(o * o, axis=-1, keepdims=True) + eps) * g_ref[...]
        o_ref[qi * tq:(qi + 1) * tq, :] = (o * (1.0 - lambda_init)).astype(o_ref.dtype)


def _diff_prompt(cfg, lp, q, k, v, subln):
    m, wdt = q.shape
    hw = 2 * cfg.diff_head_dim
    s_len = cfg.seq
    blk = pl.BlockSpec((s_len, hw), lambda b, h: (b, h))
    return pl.pallas_call(
        functools.partial(_diff_prompt_kernel, tq=_tile(s_len, 256), d=cfg.diff_head_dim,
                          lambda_init=cfg.lambda_init, eps=cfg.subln_eps),
        out_shape=jax.ShapeDtypeStruct((m, wdt), BF16),
        grid=(m // s_len, wdt // hw),
        in_specs=[pl.BlockSpec(lp.shape, lambda b, h: (0, 0)), blk, blk, blk,
                  pl.BlockSpec((1, hw), lambda b, h: (0, 0))],
        out_specs=blk,
        compiler_params=_params("parallel", "parallel"),
        name="diff_prompt",
    )(lp, q, k, v, subln.reshape(1, hw))


def _diff_scores_kernel(pt_ref, q_ref, k_ref, s_ref, *, scale):
    del pt_ref
    q = q_ref[0]
    ps = k_ref.shape[1]
    lane = lax.broadcasted_iota(jnp.int32, (q.shape[0], ps), 1)

    def body(t, s):
        col = jnp.sum(k_ref[0, t] * q, axis=-1, keepdims=True)
        return jnp.where(lane == t, col, s)

    s = lax.fori_loop(0, ps, body, jnp.zeros((q.shape[0], ps), F32), unroll=8)
    s_ref[0] = s * scale


def _diff_weights_kernel(lp_ref, sa_ref, qa_ref, ka_ref, sb_ref, qb_ref, kb_ref, w_ref, wn_ref, *,
                         scale, lambda_init):
    lam = _diff_lambda(lp_ref[...], lambda_init)
    p, pn = [], []
    for s_r, q_r, k_r in ((sa_ref, qa_ref, ka_ref), (sb_ref, qb_ref, kb_ref)):
        s = s_r[0]
        sn = jnp.sum(q_r[0] * k_r[0], axis=-1, keepdims=True) * scale
        mx = jnp.maximum(jnp.max(s, axis=-1, keepdims=True), sn)
        e = jnp.exp(s - mx)
        en = jnp.exp(sn - mx)
        den = jnp.sum(e, axis=-1, keepdims=True) + en
        p.append(e / den)
        pn.append(en / den)
    w_ref[0] = p[0] - lam * p[1]
    wn_ref[0] = pn[0] - lam * pn[1]


def _diff_pv_kernel(pt_ref, w_ref, wn_ref, v_ref, vn_ref, subln_ref, o_ref, acc_ref, *, lambda_init, eps):
    del pt_ref
    pg = pl.program_id(1)

    @pl.when(pg == 0)
    def _():
        acc_ref[...] = wn_ref[0] * vn_ref[0]

    w = w_ref[0]
    n_acc = 4
    accs = [acc_ref[...]] + [jnp.zeros(acc_ref.shape, F32)] * (n_acc - 1)
    for t in range(w.shape[1]):
        accs[t % n_acc] = accs[t % n_acc] + w[:, t:t + 1] * v_ref[0, t]
    acc_ref[...] = (accs[0] + accs[1]) + (accs[2] + accs[3])

    @pl.when(pg == pl.num_programs(1) - 1)
    def _():
        o = acc_ref[...]
        ms = jnp.mean(o * o, axis=-1, keepdims=True)
        o_ref[0] = o * lax.rsqrt(ms + eps) * subln_ref[...] * (1.0 - lambda_init)


def _diff_sample(cfg, lp, q, k, v, cache_k, cache_v, page_table, subln):
    db, wdt = q.shape
    h = cfg.diff_heads
    d = cfg.diff_head_dim
    hw = 2 * d
    ps = cfg.page_size
    n_pages = page_table.shape[1]
    n_past = n_pages * ps
    scale = d ** -0.5
    q3, k3 = q.reshape(db, 2 * h, d), k.reshape(db, 2 * h, d)
    v3 = v.reshape(db, h, hw)
    rows = lambda nr, nc: pl.BlockSpec((1, nr, nc), lambda b, p, pt: (b, 0, 0))
    page = lambda nr, nc: pl.BlockSpec((1, ps, nr, nc), lambda b, p, pt: (pt[b, p], 0, 0, 0))
    s = pl.pallas_call(
        functools.partial(_diff_scores_kernel, scale=scale),
        out_shape=jax.ShapeDtypeStruct((db, 2 * h, n_past), F32),
        grid_spec=pltpu.PrefetchScalarGridSpec(
            num_scalar_prefetch=1, grid=(db, n_pages),
            in_specs=[rows(2 * h, d), page(2 * h, d)],
            out_specs=pl.BlockSpec((1, 2 * h, ps), lambda b, p, pt: (b, 0, p))),
        compiler_params=_params("parallel", "arbitrary"),
        name="diff_scores",
    )(page_table, q3, cache_k)
    maps = lambda a: [a.reshape(db, h, 2, a.shape[-1])[:, :, c] for c in range(2)]
    (qa, qb), (ka, kb), (sa, sb) = maps(q3), maps(k3), maps(s)
    headb = pl.BlockSpec((1, h, d), lambda b: (b, 0, 0))
    pastb = pl.BlockSpec((1, h, n_past), lambda b: (b, 0, 0))
    w, wn = pl.pallas_call(
        functools.partial(_diff_weights_kernel, scale=scale, lambda_init=cfg.lambda_init),
        out_shape=(jax.ShapeDtypeStruct((db, h, n_past), F32), jax.ShapeDtypeStruct((db, h, 1), F32)),
        grid=(db,),
        in_specs=[pl.BlockSpec(lp.shape, lambda b: (0, 0)), pastb, headb, headb, pastb, headb, headb],
        out_specs=(pastb, pl.BlockSpec((1, h, 1), lambda b: (b, 0, 0))),
        compiler_params=_params("parallel"),
        name="diff_weights",
    )(lp, sa, qa, ka, sb, qb, kb)
    o = pl.pallas_call(
        functools.partial(_diff_pv_kernel, lambda_init=cfg.lambda_init, eps=cfg.subln_eps),
        out_shape=jax.ShapeDtypeStruct((db, h, hw), F32),
        grid_spec=pltpu.PrefetchScalarGridSpec(
            num_scalar_prefetch=1, grid=(db, n_pages),
            in_specs=[pl.BlockSpec((1, h, ps), lambda b, p, pt: (b, 0, p)),
                      rows(h, 1), page(h, hw), rows(h, hw),
                      pl.BlockSpec((1, hw), lambda b, p, pt: (0, 0))],
            out_specs=rows(h, hw),
            scratch_shapes=[pltpu.VMEM((h, hw), F32)]),
        compiler_params=_params("parallel", "arbitrary"),
        name="diff_pv",
    )(page_table, w, wn, cache_v, v3, subln.reshape(1, hw))
    return o.reshape(db, wdt)


def _top_mask_lanes(gate, n_top):
    nb = gate.shape[1]
    lane = lax.broadcasted_iota(jnp.int32, gate.shape, 1).astype(F32)
    sel = jnp.zeros(gate.shape, F32)
    for _ in range(n_top):
        mx = jnp.max(gate, axis=1, keepdims=True)
        idx = jnp.min(jnp.where(gate == mx, lane, float(nb)), axis=1, keepdims=True)
        hit = (lane == idx) & (mx > 0.5 * NEG_BIG)
        sel = jnp.where(hit, 1.0, sel)
        gate = jnp.where(lane == idx, NEG_BIG, gate)
    return sel


def _moba_prompt_kernel(q_ref, k_ref, v_ref, ext_ref, o_ref, *, bs, n_top):
    s_len, d = q_ref.shape
    nb = s_len // bs
    kf = k_ref[...]
    vb = v_ref[...].astype(BF16)
    kmean = jnp.sum(kf.reshape(nb, bs, d), axis=1) / bs
    kmean = jnp.concatenate([kmean, jnp.zeros((LANES - nb, d), F32)], axis=0)
    k_aug = jnp.concatenate([kf.astype(BF16), ext_ref[...]], axis=1)
    scale = d ** -0.5
    nt = (((1,), (1,)), ((), ()))
    causal = (lax.broadcasted_iota(jnp.int32, (bs, bs), 1) <= lax.broadcasted_iota(jnp.int32, (bs, bs), 0))
    blk = lax.broadcasted_iota(jnp.int32, (bs, LANES), 1)
    for qi in range(nb):
        lo, nk = qi * bs, (qi + 1) * bs
        qf = q_ref[lo:nk, :]
        allowed = blk == qi
        if qi > 0:
            gate = lax.dot_general(qf, kmean, nt, precision=lax.Precision.HIGHEST,
                                   preferred_element_type=F32)
            allowed = allowed | (_top_mask_lanes(jnp.where(blk < qi, gate, NEG_BIG), n_top) > 0.5)
        bias = jnp.where(allowed, 0.0, NEG_BIG).astype(BF16)
        q_aug = jnp.concatenate([(qf * scale).astype(BF16), bias], axis=1)
        s = lax.dot_general(q_aug, k_aug[:nk], nt, preferred_element_type=F32)
        s_own = jnp.where(causal, s[:, lo:nk], NEG_BIG)
        mx = jnp.max(s_own, axis=-1, keepdims=True)
        if qi > 0:
            mx = jnp.maximum(mx, jnp.max(s[:, :lo], axis=-1, keepdims=True))
        e_own = jnp.exp(s_own - mx)
        den = jnp.sum(e_own, axis=-1, keepdims=True)
        o = jnp.dot(e_own.astype(BF16), vb[lo:nk], preferred_element_type=F32)
        if qi > 0:
            e_past = jnp.exp(s[:, :lo] - mx)
            den = den + jnp.sum(e_past, axis=-1, keepdims=True)
            o = o + jnp.dot(e_past.astype(BF16), vb[:lo], preferred_element_type=F32)
        o_ref[lo:nk, :] = (o / den).astype(o_ref.dtype)


def _moba_prompt(cfg, q, k, v):
    m, wdt = q.shape
    d = cfg.moba_head_dim
    s_len = cfg.seq
    bs = cfg.moba_block
    nb = s_len // bs
    assert nb <= LANES
    onehot = ((jnp.arange(s_len) // bs)[:, None] == jnp.arange(LANES)[None, :]).astype(BF16)
    blk = pl.BlockSpec((s_len, d), lambda b, h: (b, h))
    return pl.pallas_call(
        functools.partial(_moba_prompt_kernel, bs=bs, n_top=min(cfg.moba_topk, nb)),
        out_shape=jax.ShapeDtypeStruct((m, wdt), BF16),
        grid=(m // s_len, wdt // d),
        in_specs=[blk, blk, blk, pl.BlockSpec((s_len, LANES), lambda b, h: (0, 0))],
        out_specs=blk,
        compiler_params=_params("parallel", "parallel"),
        name="moba_prompt",
    )(q, k, v, onehot)


def _moba_select_kernel(pt_ref, q_ref, kn_ref, *refs, ppb, bs, n_past_blocks, own_block, n_top):
    del pt_ref
    k_refs, sel_ref, gate_ref = refs[:ppb], refs[ppb], refs[ppb + 1]
    blk = pl.program_id(1)
    q = q_ref[0]
    lane = lax.broadcasted_iota(jnp.int32, gate_ref.shape, 1)

    @pl.when(blk == 0)
    def _():
        gate_ref[...] = jnp.zeros(gate_ref.shape, F32)

    def body(t, tot):
        for r in k_refs:
            tot = tot + r[0, t]
        return tot

    tot = lax.fori_loop(0, k_refs[0].shape[1], body, jnp.zeros(q.shape, F32), unroll=8)
    g = jnp.sum(tot * q, axis=-1, keepdims=True) / bs
    gate_ref[...] = jnp.where(lane == blk, g, gate_ref[...])

    @pl.when(blk == pl.num_programs(1) - 1)
    def _():
        g_own = jnp.sum(kn_ref[0] * q, axis=-1, keepdims=True) / bs
        gate = gate_ref[...] + jnp.where(lane == own_block, g_own, 0.0)
        gate = jnp.where((lane < own_block) & (lane < n_past_blocks), gate, NEG_BIG)
        lanef = lane.astype(F32)
        out = jnp.zeros(gate.shape, F32)
        for j in range(n_top):
            mx = jnp.max(gate, axis=1, keepdims=True)
            idx = jnp.min(jnp.where(gate == mx, lanef, float(LANES)), axis=1, keepdims=True)
            out = jnp.where(lane == j, idx, out)
            gate = jnp.where(lanef == idx, NEG_BIG, gate)
        sel_ref[0] = out.astype(jnp.int32)


def _moba_sample_attn_kernel(sel_ref, pt_ref, q_ref, kn_ref, vn_ref, *refs, n_pg, scale, hg):
    del sel_ref, pt_ref
    k_refs, v_refs, o_ref = refs[:n_pg], refs[n_pg:2 * n_pg], refs[2 * n_pg]
    hh = pl.program_id(1) % hg

    def pick(r):
        acc = jnp.where(hh == 0, 1.0, 0.0) * r[0, :, 0, :]
        for j in range(1, hg):
            acc = acc + jnp.where(hh == j, 1.0, 0.0) * r[0, :, j, :]
        return acc

    k_pages = [pick(r) for r in k_refs]
    v_pages = [pick(r) for r in v_refs]
    qb = jnp.broadcast_to(q_ref[0, 0], (SUBLANES, q_ref.shape[3])).astype(BF16)
    nt = (((1,), (1,)), ((), ()))
    s = [lax.dot_general(qb, x.astype(BF16), nt, preferred_element_type=F32) * scale for x in k_pages]
    knb = jnp.broadcast_to(kn_ref[0, 0], qb.shape).astype(BF16).astype(F32)
    sn = jnp.sum(qb.astype(F32) * knb, axis=-1, keepdims=True) * scale
    mx = sn
    for x in s:
        mx = jnp.maximum(mx, jnp.max(x, axis=-1, keepdims=True))
    e = [jnp.exp(x - mx) for x in s]
    en = jnp.exp(sn - mx)
    den = en
    for x in e:
        den = den + jnp.sum(x, axis=-1, keepdims=True)
    o = (en / den).astype(BF16).astype(F32) * vn_ref[0, 0].astype(BF16).astype(F32)
    for x, vp in zip(e, v_pages):
        o = o + jnp.dot((x / den).astype(BF16), vp.astype(BF16), preferred_element_type=F32)
    o_ref[0, 0] = o[0:1]


def _moba_sample(cfg, q, k, v, cache_k, cache_v, page_table):
    db, wdt = q.shape
    d = cfg.moba_head_dim
    nh = cfg.moba_heads
    ps, bs = cfg.page_size, cfg.moba_block
    ppb = bs // ps
    n_pages = page_table.shape[1]
    n_pool = cache_k.shape[0]
    assert n_pages % ppb == 0 and cfg.past_len == n_pages * ps
    nbp = n_pages // ppb
    own = cfg.past_len // bs
    n_top = cfg.moba_topk
    assert nbp >= n_top and own == nbp and own < LANES
    hg = min(SUBLANES, nh)
    assert nh % hg == 0
    q3, k3 = q.reshape(db, nh, d), k.reshape(db, nh, d)

    def page_spec(r):
        return pl.BlockSpec((1, ps, nh, d), lambda b, n, pt: (pt[b, n * ppb + r], 0, 0, 0))

    allh = pl.BlockSpec((1, nh, d), lambda b, n, pt: (b, 0, 0))
    sel = pl.pallas_call(
        functools.partial(_moba_select_kernel, ppb=ppb, bs=float(bs), n_past_blocks=nbp, own_block=own,
                          n_top=n_top),
        out_shape=jax.ShapeDtypeStruct((db, nh, LANES), jnp.int32),
        grid_spec=pltpu.PrefetchScalarGridSpec(
            num_scalar_prefetch=1, grid=(db, nbp),
            in_specs=[allh, allh] + [page_spec(r) for r in range(ppb)],
            out_specs=pl.BlockSpec((1, nh, LANES), lambda b, n, pt: (b, 0, 0)),
            scratch_shapes=[pltpu.VMEM((nh, LANES), F32)]),
        compiler_params=_params("parallel", "arbitrary"),
        name="moba_select",
    )(page_table, q3, k3, *([cache_k] * ppb))
    sel_flat = sel[:, :, :n_top].reshape(-1)
    pt_flat = page_table.reshape(-1)
    n_pg = n_top * ppb

    def sel_page(jp):
        j, r = divmod(jp, ppb)
        return pl.BlockSpec(
            (1, ps, hg, d),
            lambda b, h, sl, pt: (pt[b * n_pages + sl[(b * nh + h) * n_top + j] * ppb + r], 0, h // hg, 0))

    q4, k4, v4 = (a.reshape(db, nh, 1, d) for a in (q, k, v))
    headb = pl.BlockSpec((1, 1, 1, d), lambda b, h, sl, pt: (b, h, 0, 0))
    o = pl.pallas_call(
        functools.partial(_moba_sample_attn_kernel, n_pg=n_pg, scale=d ** -0.5, hg=hg),
        out_shape=jax.ShapeDtypeStruct((db, nh, 1, d), F32),
        grid_spec=pltpu.PrefetchScalarGridSpec(
            num_scalar_prefetch=2, grid=(db, nh),
            in_specs=[headb, headb, headb] + [sel_page(jp) for jp in range(n_pg)] * 2,
            out_specs=headb),
        compiler_params=_params("parallel", "parallel"),
        name="moba_sample_attn",
    )(sel_flat, pt_flat, q4, k4, v4, *([cache_k] * n_pg), *([cache_v] * n_pg))
    return o.reshape(db, wdt)


def _rope_tables(cfg, pos, d):
    half = d // 2
    inv = cfg.rope_theta ** (-jnp.arange(half, dtype=F32) / half)
    ang = pos.astype(F32)[:, None] * inv[None, :]
    cos, sin = jnp.cos(ang), jnp.sin(ang)
    return jnp.concatenate([cos, cos], axis=-1), jnp.concatenate([-sin, sin], axis=-1)


def _rwkv_layout(cfg):
    c = cfg.rwkv_width
    segs, o = [], 0
    for wd in (c, c, c, cfg.lora_w, cfg.lora_a, cfg.lora_g):
        segs.append((o, wd, _rup(wd, LANES)))
        o += wd
    return segs, o


def _pad_cols(cfg, a):
    segs, _ = _rwkv_layout(cfg)
    parts = []
    for o, wd, pw in segs:
        x = a[..., o:o + wd]
        if pw != wd:
            x = jnp.pad(x, [(0, 0)] * (a.ndim - 1) + [(0, pw - wd)])
        parts.append(x)
    return jnp.concatenate(parts, axis=-1)


def _unpad_cols(cfg, a):
    segs, _ = _rwkv_layout(cfg)
    parts, o = [], 0
    for _, wd, pw in segs:
        parts.append(a[..., o:o + wd])
        o += pw
    return jnp.concatenate(parts, axis=-1)


def _pad_rows(a, rows):
    return jnp.pad(a, ((0, rows - a.shape[0]), (0, 0)))


def _state_to_pairs(cfg, s):
    b, h, n, _ = s.shape
    return jnp.transpose(s.reshape(b, h // 2, 2, n, n), (0, 1, 3, 2, 4)).reshape(b, h // 2, n, 2 * n)


def _pairs_to_state(cfg, s):
    b, hp, n, _ = s.shape
    return jnp.transpose(s.reshape(b, hp, n, 2, n), (0, 1, 3, 2, 4)).reshape(b, 2 * hp, n, n)


def _forward(cfg, x_prompt, x_sample, cache_k0, cache_v0, cache_k1, cache_v1, page_table,
             state_wkv0, state_shift0, state_conv,
             norm_mix, norm_ffn, norm_final, w_in0, w_out0,
             rwkv_mu, rwkv_w0, rwkv_w2, rwkv_a0, rwkv_a2, rwkv_g2, rwkv_kk, rwkv_ka, rwkv_rk,
             rwkv_ln_w, rwkv_ln_b, diff_lq1, diff_lk1, diff_lq2, diff_lk2, diff_subln,
             w_in1, w_out1, ffn_w_up, ffn_conv_w, ffn_conv_b, ffn_w_down):
    bsz, s_len, dm = x_prompt.shape
    db = x_sample.shape[0]
    c = cfg.rwkv_width
    dw = dm - c
    segs, rproj = _rwkv_layout(cfg)
    n_heads = c // cfg.rwkv_head_dim

    w_r = _pad_cols(cfg, w_in0[:, :rproj]).astype(BF16)
    w_q, w_k, w_v = (w_in0[:, rproj + i * dw:rproj + (i + 1) * dw].astype(BF16) for i in range(3))
    w_out0_b = w_out0.astype(BF16)
    w_in1_b = w_in1.astype(BF16)
    w_out1_b = w_out1.astype(BF16)
    w_up_b = ffn_w_up.astype(BF16)
    w_down_b = ffn_w_down.astype(BF16)
    lw_p, la_p = _rup(cfg.lora_w, LANES), _rup(cfg.lora_a, LANES)
    row = lambda a: a.reshape(1, -1)
    prm = dict(mu=row(_pad_cols(cfg, rwkv_mu)), w0=row(rwkv_w0), a0=row(rwkv_a0), kk=row(rwkv_kk),
               ka=row(rwkv_ka), w2=_pad_rows(rwkv_w2, lw_p).astype(BF16),
               a2=_pad_rows(rwkv_a2, la_p).astype(BF16),
               g2=_pad_rows(rwkv_g2, _rup(cfg.lora_g, LANES)).astype(BF16),
               ln_w=row(rwkv_ln_w), ln_b=row(rwkv_ln_b), rk=row(rwkv_rk))
    lp = jnp.stack([diff_lq1, diff_lk1, diff_lq2, diff_lk2])
    rope_p = _rope_tables(cfg, jnp.arange(s_len, dtype=jnp.int32), cfg.diff_head_dim)
    rope_s = _rope_tables(cfg, jnp.full((db,), cfg.past_len, jnp.int32), cfg.diff_head_dim)

    xp = x_prompt.reshape(bsz * s_len, dm)
    xs = x_sample.reshape(db, dm)
    out = {}
    conv_p, conv_s = [], []
    for layer in range(2):
        hp = _rmsnorm(xp, norm_mix[layer], cfg.norm_eps, BF16)
        hs = _rmsnorm(xs, norm_mix[layer], cfg.norm_eps, BF16)
        if layer == 0:
            mix = {}
            for tag, h, n_seq, t_len, rope in (("p", hp, bsz, s_len, rope_p), ("s", hs, db, 1, rope_s)):
                rows = n_seq * t_len
                p = _matmul(h, w_r)
                if tag == "p":
                    p3 = p.reshape(n_seq, t_len, -1)
                    p_prev = jnp.concatenate([jnp.zeros_like(p3[:, :1]), p3[:, :-1]], axis=1).reshape(rows, -1)
                    s0 = jnp.zeros((n_seq, n_heads // 2, cfg.rwkv_head_dim, LANES), F32)
                    shift = p3[:, -1]
                else:
                    p_prev = _pad_cols(cfg, state_shift0)
                    s0 = _state_to_pairs(cfg, state_wkv0)
                    shift = p
                r, w, k, v, kk, b, g = _rwkv_prep(cfg, p, p_prev, prm)
                y, sf = _rwkv_scan(cfg, r, w, k, v, kk, b, s0, n_seq, t_len)
                a_out = _rwkv_post(cfg, y, r, k, v, g, prm)
                q0 = _matmul(h, w_q, rope=rope, rows_per_seq=t_len if tag == "p" else db)
                k0 = _matmul(h, w_k, rope=rope, rows_per_seq=t_len if tag == "p" else db)
                v0 = _matmul(h, w_v)
                if tag == "p":
                    b_out = _diff_prompt(cfg, lp, q0, k0, v0, diff_subln)
                else:
                    b_out = _diff_sample(cfg, lp, q0, k0, v0, cache_k0, cache_v0, page_table,
                                         diff_subln).astype(BF16)
                mix[tag] = (a_out, b_out)
                out["k0" + tag] = k0.reshape(n_seq, t_len, 2 * cfg.diff_heads, cfg.diff_head_dim)
                out["v0" + tag] = v0.reshape(n_seq, t_len, cfg.diff_heads, 2 * cfg.diff_head_dim)
                out["wkv" + tag] = _pairs_to_state(cfg, sf)
                out["shift" + tag] = _unpad_cols(cfg, shift)
            xp = _matmul(mix["p"][0], w_out0_b, x2=mix["p"][1], w2=w_out0_b, w2_row0=c, res=xp)
            xs = _matmul(mix["s"][0], w_out0_b, x2=mix["s"][1], w2=w_out0_b, w2_row0=c, res=xs)
        else:
            qkv = []
            for tag, h, n_seq, t_len, rope in (("p", hp, bsz, s_len, rope_p), ("s", hs, db, 1, rope_s)):
                rps = t_len if tag == "p" else db
                q1 = _matmul(h, w_in1_b, cols=(0, dm), rope=rope, rows_per_seq=rps)
                k1 = _matmul(h, w_in1_b, cols=(dm, dm), rope=rope, rows_per_seq=rps)
                v1 = _matmul(h, w_in1_b, cols=(2 * dm, dm))
                out["k1" + tag] = k1.reshape(n_seq, t_len, cfg.moba_heads, cfg.moba_head_dim)
                out["v1" + tag] = v1.reshape(n_seq, t_len, cfg.moba_heads, cfg.moba_head_dim)
                qkv.append((q1, k1, v1))
            mp = _moba_prompt(cfg, *qkv[0])
            ms = _moba_sample(cfg, *qkv[1], cache_k1, cache_v1, page_table).astype(BF16)
            xp = _matmul(mp, w_out1_b, res=xp)
            xs = _matmul(ms, w_out1_b, res=xs)
        hp = _rmsnorm(xp, norm_ffn[layer], cfg.norm_eps, BF16)
        hs = _rmsnorm(xs, norm_ffn[layer], cfg.norm_eps, BF16)
        fp, cp = _ffn_up(hp, w_up_b[layer], ffn_conv_w[layer], ffn_conv_b[layer], s_len)
        prev = state_conv[layer]
        fs, gs = _ffn_up_step(hs, w_up_b[layer], ffn_conv_w[layer], ffn_conv_b[layer], prev[:, 0], prev[:, 1])
        xp = _matmul(fp, w_down_b[layer], res=xp, tm=512, tn=256)
        xs = _matmul(fs, w_down_b[layer], res=xs, tm=512, tn=256)
        conv_p.append(cp)
        conv_s.append(jnp.stack([prev[:, 1], gs], axis=1))
    y_prompt = _rmsnorm(xp, norm_final, cfg.norm_eps, F32).reshape(bsz, s_len, dm)
    y_sample = _rmsnorm(xs, norm_final, cfg.norm_eps, F32).reshape(db, 1, dm)
    return (y_prompt, y_sample, out["k0p"], out["v0p"], out["k0s"], out["v0s"],
            out["wkvp"], out["wkvs"], out["shiftp"], out["shifts"],
            out["k1p"], out["v1p"], out["k1s"], out["v1s"], jnp.stack(conv_p), jnp.stack(conv_s))


def kernel(x_prompt, x_sample, cache_k0, cache_v0, cache_k1, cache_v1, page_table, state_wkv0, state_shift0, state_conv, norm_mix, norm_ffn, norm_final, w_in0, w_out0, rwkv_mu, rwkv_w0, rwkv_w2, rwkv_a0, rwkv_a2, rwkv_g2, rwkv_kk, rwkv_ka, rwkv_rk, rwkv_ln_w, rwkv_ln_b, diff_lq1, diff_lk1, diff_lq2, diff_lk2, diff_subln, w_in1, w_out1, ffn_w_up, ffn_conv_w, ffn_conv_b, ffn_w_down):
    return _forward(_CFG, x_prompt, x_sample, cache_k0, cache_v0, cache_k1, cache_v1, page_table,
                    state_wkv0, state_shift0, state_conv, norm_mix, norm_ffn, norm_final, w_in0, w_out0,
                    rwkv_mu, rwkv_w0, rwkv_w2, rwkv_a0, rwkv_a2, rwkv_g2, rwkv_kk, rwkv_ka, rwkv_rk,
                    rwkv_ln_w, rwkv_ln_b, diff_lq1, diff_lk1, diff_lq2, diff_lk2, diff_subln,
                    w_in1, w_out1, ffn_w_up, ffn_conv_w, ffn_conv_b, ffn_w_down)
```

```python
import functools
import math
from typing import NamedTuple

import jax
import jax.numpy as jnp
from jax import lax
from jax.experimental import pallas as pl
from jax.experimental.pallas import tpu as pltpu

F32 = jnp.float32
BF16 = jnp.bfloat16
LANES = 128
SUBLANES = 8
NEG_BIG = -1e30
VMEM_LIMIT = 56 * 1024 * 1024


class Cfg(NamedTuple):
    d_model: int
    batch: int
    seq: int
    dec_batch: int
    past_len: int
    page_size: int
    rwkv_width: int
    rwkv_head_dim: int
    lora_w: int
    lora_a: int
    lora_g: int
    diff_heads: int
    diff_head_dim: int
    moba_heads: int
    moba_head_dim: int
    moba_block: int
    moba_topk: int
    d_ff: int
    rope_theta: float
    norm_eps: float
    subln_eps: float
    gn_eps: float
    lambda_init: float


def _make_cfg(d_model=4096, batch=4, seq=2048, dec_batch=8, past_len=16384, page_size=128, d_ff=11008,
              diff_heads=8, moba_block=256):
    rw = d_model // 2
    lw = max(32, int(round(1.8 * rw ** 0.5 / 32)) * 32)
    lg = max(32, int(round(0.6 * rw ** 0.8 / 32)) * 32)
    return Cfg(d_model=d_model, batch=batch, seq=seq, dec_batch=dec_batch, past_len=past_len,
               page_size=page_size, rwkv_width=rw, rwkv_head_dim=64, lora_w=lw, lora_a=lw, lora_g=lg,
               diff_heads=diff_heads, diff_head_dim=(d_model - rw) // (2 * diff_heads),
               moba_heads=d_model // 128, moba_head_dim=128, moba_block=moba_block, moba_topk=3,
               d_ff=d_ff, rope_theta=10000.0, norm_eps=1e-6, subln_eps=1e-5, gn_eps=64e-5,
               lambda_init=0.8 - 0.6 * math.exp(-0.3 * 0))


_CFG = _make_cfg()


def _rup(n, m):
    return -(-n // m) * m


def _tile(n, pref):
    if n <= pref:
        return n
    t = pref
    while n % t:
        t -= SUBLANES
    return t


def _params(*sem):
    return pltpu.CompilerParams(dimension_semantics=sem, vmem_limit_bytes=VMEM_LIMIT)


def _split_bf16(x):
    hi = x.astype(BF16)
    lo = (x - hi.astype(F32)).astype(BF16)
    return hi, lo


def _seg_dot(x, ind2):
    hi, lo = _split_bf16(x)
    return jnp.dot(jnp.concatenate([hi, lo], axis=1), ind2, preferred_element_type=F32)


def _rms_kernel(x_ref, g_ref, o_ref, *, eps):
    x = x_ref[...]
    ms = jnp.mean(x * x, axis=-1, keepdims=True)
    o_ref[...] = (x * lax.rsqrt(ms + eps) * g_ref[...]).astype(o_ref.dtype)


def _rmsnorm(x, g, eps, out_dtype):
    m, d = x.shape
    tm = _tile(m, 256)
    return pl.pallas_call(
        functools.partial(_rms_kernel, eps=eps),
        out_shape=jax.ShapeDtypeStruct((m, d), out_dtype),
        grid=(m // tm,),
        in_specs=[pl.BlockSpec((tm, d), lambda i: (i, 0)), pl.BlockSpec((1, d), lambda i: (0, 0))],
        out_specs=pl.BlockSpec((tm, d), lambda i: (i, 0)),
        compiler_params=_params("parallel"),
        name="rmsnorm",
    )(x, g.reshape(1, d))


def _mm_kernel(*refs, has_res, has_rope, two_lhs):
    it = iter(refs)
    x_ref = next(it)
    x2_ref = next(it) if two_lhs else None
    w_ref = next(it)
    w2_ref = next(it) if two_lhs else None
    res_ref = next(it) if has_res else None
    cos_ref = next(it) if has_rope else None
    sin_ref = next(it) if has_rope else None
    o_ref = next(it)
    acc = jnp.dot(x_ref[...], w_ref[...].astype(BF16), preferred_element_type=F32)
    if two_lhs:
        acc = acc + jnp.dot(x2_ref[...], w2_ref[...].astype(BF16), preferred_element_type=F32)
    if has_rope:
        cos = cos_ref[...]
        sin = sin_ref[...]
        for s in range(acc.shape[1] // LANES):
            a = acc[:, s * LANES:(s + 1) * LANES]
            o_ref[:, s * LANES:(s + 1) * LANES] = (
                a * cos + pltpu.roll(a, LANES // 2, axis=1) * sin).astype(o_ref.dtype)
    else:
        if has_res:
            acc = acc + res_ref[...]
        o_ref[...] = acc.astype(o_ref.dtype)


def _matmul(x, w, *, w_lead=None, cols=None, x2=None, w2=None, w2_row0=0, res=None, rope=None,
            rows_per_seq=None, out_dtype=F32, tm=1024, tn=512):
    m, k = x.shape
    col0, n = cols if cols is not None else (0, w.shape[-1])
    tm = _tile(m if rope is None else rows_per_seq, tm)
    tn = _tile(n, tn)
    assert col0 % tn == 0
    jb = col0 // tn
    two = x2 is not None
    in_specs = [pl.BlockSpec((tm, k), lambda i, j: (i, 0))]
    args = [x]
    if two:
        in_specs.append(pl.BlockSpec((tm, x2.shape[1]), lambda i, j: (i, 0)))
        args.append(x2)
    if w.ndim == 3:
        in_specs.append(pl.BlockSpec((None, k, tn), lambda i, j: (w_lead, 0, jb + j)))
    else:
        in_specs.append(pl.BlockSpec((k, tn), lambda i, j: (0, jb + j)))
    args.append(w)
    if two:
        k2 = x2.shape[1]
        assert w2_row0 % k2 == 0
        in_specs.append(pl.BlockSpec((k2, tn), lambda i, j: (w2_row0 // k2, jb + j)))
        args.append(w2)
    if res is not None:
        in_specs.append(pl.BlockSpec((tm, tn), lambda i, j: (i, j)))
        args.append(res)
    if rope is not None:
        tps = rows_per_seq // tm
        for t in rope:
            in_specs.append(pl.BlockSpec((tm, LANES), lambda i, j: (i % tps, 0)))
            args.append(t)
    return pl.pallas_call(
        functools.partial(_mm_kernel, has_res=res is not None, has_rope=rope is not None, two_lhs=two),
        out_shape=jax.ShapeDtypeStruct((m, n), out_dtype),
        grid=(m // tm, n // tn),
        in_specs=in_specs,
        out_specs=pl.BlockSpec((tm, tn), lambda i, j: (i, j)),
        compiler_params=_params("parallel", "parallel"),
        name="matmul",
    )(*args)


def _ffn_up_kernel(x_ref, wg_ref, wu_ref, cw_ref, cb_ref, h_ref, cs_ref, carry_ref, *, tiles_per_seq):
    i = pl.program_id(0)
    j = pl.program_id(1)
    x = x_ref[...]
    g = jnp.dot(x, wg_ref[...].astype(BF16), preferred_element_type=F32)
    u = jnp.dot(x, wu_ref[...].astype(BF16), preferred_element_type=F32)
    tm = g.shape[0]

    @pl.when(i % tiles_per_seq == 0)
    def _():
        carry_ref[j] = jnp.zeros(carry_ref.shape[1:], F32)

    carry = carry_ref[j]
    row = lax.broadcasted_iota(jnp.int32, g.shape, 0)
    g1 = jnp.where(row == 0, carry[1:2], pltpu.roll(g, 1, axis=0))
    g2 = jnp.where(row == 0, carry[0:1], jnp.where(row == 1, carry[1:2], pltpu.roll(g, 2, axis=0)))
    cw = cw_ref[...]
    gc = cb_ref[...] + cw[0:1] * g2
    gc = gc + cw[1:2] * g1
    gc = gc + cw[2:3] * g
    h_ref[...] = (gc * jax.nn.sigmoid(gc) * u).astype(h_ref.dtype)
    last2 = g[tm - 2:tm, :]
    carry_ref[j, 0:2, :] = last2
    cs_ref[0] = last2


def _ffn_up(x, w_up, layer, conv_w, conv_b, rows_per_seq, tm=1024, tn=256):
    m, d = x.shape
    f = w_up.shape[2] // 2
    tm = _tile(rows_per_seq, tm)
    tn = _tile(f, tn)
    nj = f // tn
    tps = rows_per_seq // tm
    h, tails = pl.pallas_call(
        functools.partial(_ffn_up_kernel, tiles_per_seq=tps),
        out_shape=(jax.ShapeDtypeStruct((m, f), BF16),
                   jax.ShapeDtypeStruct((m // tm, 2, f), F32)),
        grid=(m // tm, nj),
        in_specs=[pl.BlockSpec((tm, d), lambda i, j: (i, 0)),
                  pl.BlockSpec((None, d, tn), lambda i, j: (layer, 0, j)),
                  pl.BlockSpec((None, d, tn), lambda i, j: (layer, 0, nj + j)),
                  pl.BlockSpec((3, tn), lambda i, j: (0, j)),
                  pl.BlockSpec((1, tn), lambda i, j: (0, j))],
        out_specs=(pl.BlockSpec((tm, tn), lambda i, j: (i, j)),
                   pl.BlockSpec((1, 2, tn), lambda i, j: (i, 0, j))),
        scratch_shapes=[pltpu.VMEM((nj, SUBLANES, tn), F32)],
        compiler_params=_params("arbitrary", "arbitrary"),
        name="ffn_up",
    )(x, w_up, w_up, conv_w, conv_b.reshape(1, f))
    return h, tails[tps - 1::tps]


def _ffn_up_step_kernel(x_ref, wg_ref, wu_ref, cw_ref, cb_ref, p0_ref, p1_ref, h_ref, g_ref):
    x = x_ref[...]
    g = jnp.dot(x, wg_ref[...].astype(BF16), preferred_element_type=F32)
    u = jnp.dot(x, wu_ref[...].astype(BF16), preferred_element_type=F32)
    cw = cw_ref[...]
    gc = cb_ref[...] + cw[0:1] * p0_ref[...]
    gc = gc + cw[1:2] * p1_ref[...]
    gc = gc + cw[2:3] * g
    h_ref[...] = (gc * jax.nn.sigmoid(gc) * u).astype(h_ref.dtype)
    g_ref[...] = g


def _ffn_up_step(x, w_up, layer, conv_w, conv_b, prev0, prev1, tn=256):
    m, d = x.shape
    f = w_up.shape[2] // 2
    tn = _tile(f, tn)
    nj = f // tn
    row = lambda j: (0, j)
    return pl.pallas_call(
        _ffn_up_step_kernel,
        out_shape=(jax.ShapeDtypeStruct((m, f), BF16), jax.ShapeDtypeStruct((m, f), F32)),
        grid=(nj,),
        in_specs=[pl.BlockSpec((m, d), lambda j: (0, 0)),
                  pl.BlockSpec((None, d, tn), lambda j: (layer, 0, j)),
                  pl.BlockSpec((None, d, tn), lambda j: (layer, 0, nj + j)),
                  pl.BlockSpec((3, tn), row),
                  pl.BlockSpec((1, tn), row),
                  pl.BlockSpec((m, tn), row),
                  pl.BlockSpec((m, tn), row)],
        out_specs=(pl.BlockSpec((m, tn), row), pl.BlockSpec((m, tn), row)),
        compiler_params=_params("parallel"),
        name="ffn_up_step",
    )(x, w_up, w_up, conv_w, conv_b.reshape(1, f), prev0, prev1)


def _softplus(x):
    return jnp.maximum(x, 0.0) + jnp.log(1.0 + jnp.exp(-jnp.abs(x)))


def _rwkv_prep_kernel(p_ref, pp_ref, mu_ref, w0_ref, a0_ref, kk_ref, ka_ref, w2_ref, a2_ref, g2_ref, bd_ref,
                      r_ref, w_ref, k_ref, v_ref, kk_o_ref, b_ref, g_ref, *, c, lw, la):
    p = p_ref[...]
    xm = p + (pp_ref[...] - p) * mu_ref[...]
    r = xm[:, 0:c]
    k = xm[:, c:2 * c]
    v = xm[:, 2 * c:3 * c]
    o = 3 * c
    wl = xm[:, o:o + lw]
    al = xm[:, o + lw:o + lw + la]
    gl = xm[:, o + lw + la:]
    z = w0_ref[...] + jnp.dot(jnp.tanh(wl).astype(BF16), w2_ref[...], preferred_element_type=F32)
    w = -_softplus(-z) - 0.5
    a = jax.nn.sigmoid(a0_ref[...] + jnp.dot(al.astype(BF16), a2_ref[...], preferred_element_type=F32))
    g = jnp.dot(jax.nn.sigmoid(gl).astype(BF16), g2_ref[...], preferred_element_type=F32)
    kk = k * kk_ref[...]
    decay = jnp.exp(-jnp.exp(w))
    k2 = k * (1.0 + (a - 1.0) * ka_ref[...])
    bd2 = bd_ref[...]
    for s in range(c // LANES):
        sl = slice(s * LANES, (s + 1) * LANES)
        kks = kk[:, sl]
        ss = _seg_dot(kks * kks, bd2)
        kkn = kks / jnp.maximum(jnp.sqrt(ss), 1e-12)
        kk_o_ref[s] = kkn
        b_ref[s] = kkn * a[:, sl]
        r_ref[s] = r[:, sl]
        w_ref[s] = decay[:, sl]
        k_ref[s] = k2[:, sl]
        v_ref[s] = v[:, sl]
        g_ref[s] = g[:, sl]


def _head_pair_indicator(n):
    i = jnp.arange(LANES)
    bd = (i[:, None] // n == i[None, :] // n).astype(BF16)
    return jnp.concatenate([bd, bd], axis=0)


def _rwkv_prep(cfg, p, p_prev, prm):
    m = p.shape[0]
    c = cfg.rwkv_width
    pw = p.shape[1]
    lw, la = _rup(cfg.lora_w, LANES), _rup(cfg.lora_a, LANES)
    lg = pw - 3 * c - lw - la
    tm = _tile(m, 128)
    full = lambda i: (0, 0)
    rowblk = lambda i: (i, 0)
    out = jax.ShapeDtypeStruct((c // LANES, m, LANES), F32)
    return pl.pallas_call(
        functools.partial(_rwkv_prep_kernel, c=c, lw=lw, la=la),
        out_shape=(out,) * 7,
        grid=(m // tm,),
        in_specs=[pl.BlockSpec((tm, pw), rowblk), pl.BlockSpec((tm, pw), rowblk),
                  pl.BlockSpec((1, pw), full), pl.BlockSpec((1, c), full), pl.BlockSpec((1, c), full),
                  pl.BlockSpec((1, c), full), pl.BlockSpec((1, c), full),
                  pl.BlockSpec((lw, c), full), pl.BlockSpec((la, c), full), pl.BlockSpec((lg, c), full),
                  pl.BlockSpec((2 * LANES, LANES), full)],
        out_specs=(pl.BlockSpec((c // LANES, tm, LANES), lambda i: (0, i, 0)),) * 7,
        compiler_params=_params("parallel"),
        name="rwkv_prep",
    )(p, p_prev, prm["mu"], prm["w0"], prm["a0"], prm["kk"], prm["ka"], prm["w2"], prm["a2"], prm["g2"],
      _head_pair_indicator(cfg.rwkv_head_dim))


def _rwkv_scan_kernel(r_ref, w_ref, k_ref, v_ref, kk_ref, b_ref, s0_ref, bd_ref,
                      y_ref, sf_ref, ysc_ref, vt_ref, *st_refs, n_pairs, tc):
    t_blk = pl.program_id(2)

    @pl.when(t_blk == 0)
    def _():
        for p in range(n_pairs):
            st_refs[p][...] = s0_ref[0, p]

    rows = min(tc, SUBLANES)
    n = st_refs[0].shape[0]

    nt = (((1,), (1,)), ((), ()))
    row8 = lax.broadcasted_iota(jnp.int32, (SUBLANES, LANES), 0)
    lane8 = lax.broadcasted_iota(jnp.int32, (SUBLANES, LANES), 1)
    rmask = ((lane8 // n == row8 % 2) & (row8 < 4)).astype(F32)
    lane_n = lax.broadcasted_iota(jnp.int32, (n, LANES), 1)

    def group(t0):
        vs = [v_ref[p, pl.ds(t0, rows), :] for p in range(n_pairs)]
        pad = LANES - n_pairs * rows
        vt_ref[...] = jnp.transpose(
            jnp.concatenate(vs + ([jnp.zeros((pad, LANES), F32)] if pad else []), axis=0))
        for i in range(rows):
            for p in range(n_pairs):
                row = lambda ref: ref[p, pl.ds(t0 + i, 1), :]
                s = st_refs[p][...]
                z = _seg_dot(s * row(kk_ref), bd_ref[...])
                c = p * rows + i
                vb = jnp.where(lane_n < n, jnp.broadcast_to(vt_ref[0:n, c:c + 1], (n, LANES)),
                               jnp.broadcast_to(vt_ref[n:2 * n, c:c + 1], (n, LANES)))
                s = s * row(w_ref) - z * row(b_ref) + vb * row(k_ref)
                st_refs[p][...] = s
                r = row(r_ref)
                r_hi = r.astype(BF16).astype(F32)
                rm = (jnp.where(row8 < 2, r_hi, r - r_hi) * rmask).astype(BF16)
                y8 = lax.dot_general(rm, s.astype(BF16), nt, preferred_element_type=F32)
                y8 = y8[0:2] + y8[2:4]
                for hh in range(2):
                    ysc_ref[p, hh * rows + i:hh * rows + i + 1, :] = y8[hh:hh + 1]
        for p in range(n_pairs):
            y_ref[p, pl.ds(t0, rows), :] = jnp.concatenate(
                [ysc_ref[p, 0:rows, :], ysc_ref[p, rows:2 * rows, :]], axis=1)

    if tc <= SUBLANES:
        group(0)
    else:
        def body(g, carry):
            group(pl.multiple_of(g * SUBLANES, SUBLANES))
            return carry
        lax.fori_loop(0, tc // SUBLANES, body, 0)

    @pl.when(t_blk == pl.num_programs(2) - 1)
    def _():
        for p in range(n_pairs):
            sf_ref[0, p] = st_refs[p][...]


def _rwkv_scan(cfg, r, w, k, v, kk, b, s0, n_seq, t_len):
    c = cfg.rwkv_width
    n = cfg.rwkv_head_dim
    n_pairs_total = c // LANES
    pg = min(16, n_pairs_total)
    tc = _tile(t_len, 128)
    assert tc <= SUBLANES or tc % SUBLANES == 0
    bd2 = _head_pair_indicator(n)
    seq = pl.BlockSpec((pg, None, tc, LANES), lambda bi, g, t: (g, bi, t, 0))
    shp = lambda a: a.reshape(n_pairs_total, n_seq, t_len, LANES)
    st = pl.BlockSpec((1, pg, n, LANES), lambda bi, g, t: (bi, g, 0, 0))
    full = lambda bi, g, t: (0, 0)
    y, sf = pl.pallas_call(
        functools.partial(_rwkv_scan_kernel, n_pairs=pg, tc=tc),
        out_shape=(jax.ShapeDtypeStruct((n_pairs_total, n_seq, t_len, LANES), F32),
                   jax.ShapeDtypeStruct((n_seq, n_pairs_total, n, LANES), F32)),
        grid=(n_seq, n_pairs_total // pg, t_len // tc),
        in_specs=[seq] * 6 + [st, pl.BlockSpec((2 * LANES, LANES), full)],
        out_specs=(seq, st),
        scratch_shapes=[pltpu.VMEM((pg, 2 * SUBLANES, n), F32), pltpu.VMEM((LANES, LANES), F32)]
        + [pltpu.VMEM((n, LANES), F32)] * pg,
        compiler_params=_params("parallel", "parallel", "arbitrary"),
        name="rwkv_scan",
    )(shp(r), shp(w), shp(k), shp(v), shp(kk), shp(b), s0, bd2)
    return y.reshape(n_pairs_total, n_seq * t_len, LANES), sf


def _rwkv_post_kernel(y_ref, r_ref, k_ref, v_ref, g_ref, lnw_ref, lnb_ref, rk_ref, bd_ref, o_ref, *, n, eps):
    bd2 = bd_ref[...]
    for s in range(y_ref.shape[0]):
        sl = slice(s * LANES, (s + 1) * LANES)
        y = y_ref[s]
        mean = _seg_dot(y, bd2) / n
        d = y - mean
        var = _seg_dot(d * d, bd2) / n
        yn = d * lax.rsqrt(var + eps) * lnw_ref[:, sl] + lnb_ref[:, sl]
        bonus = _seg_dot(r_ref[s] * k_ref[s] * rk_ref[:, sl], bd2)
        yn = yn + bonus * v_ref[s]
        o_ref[:, sl] = (yn * g_ref[s]).astype(o_ref.dtype)


def _rwkv_post(cfg, y, r, k, v, g, prm):
    npair, m, _ = y.shape
    c = npair * LANES
    tm = _tile(m, 256)
    full = lambda i: (0, 0)
    big = pl.BlockSpec((npair, tm, LANES), lambda i: (0, i, 0))
    vec = pl.BlockSpec((1, c), full)
    return pl.pallas_call(
        functools.partial(_rwkv_post_kernel, n=float(cfg.rwkv_head_dim), eps=cfg.gn_eps),
        out_shape=jax.ShapeDtypeStruct((m, c), BF16),
        grid=(m // tm,),
        in_specs=[big] * 5 + [vec] * 3 + [pl.BlockSpec((2 * LANES, LANES), full)],
        out_specs=pl.BlockSpec((tm, c), lambda i: (i, 0)),
        compiler_params=_params("parallel"),
        name="rwkv_post",
    )(y, r, k, v, g, prm["ln_w"], prm["ln_b"], prm["rk"], _head_pair_indicator(cfg.rwkv_head_dim))


def _diff_lambda(lp, lambda_init):
    s1 = jnp.sum(lp[0:1] * lp[1:2], axis=-1, keepdims=True)
    s2 = jnp.sum(lp[2:3] * lp[3:4], axis=-1, keepdims=True)
    return jnp.exp(s1) - jnp.exp(s2) + lambda_init


def _softmax_rows(s):
    m = jnp.max(s, axis=-1, keepdims=True)
    e = jnp.exp(s - m)
    return e / jnp.sum(e, axis=-1, keepdims=True)


def _diff_prompt_kernel(lp_ref, q_ref, k_ref, v_ref, g_ref, o_ref, *, tq, d, lambda_init, eps):
    lam = _diff_lambda(lp_ref[...], lambda_init)
    s_len = q_ref.shape[0]
    kb = k_ref[...].astype(BF16)
    vb = v_ref[...].astype(BF16)
    scale = d ** -0.5
    nt = (((1,), (1,)), ((), ()))
    for qi in range(s_len // tq):
        nk = (qi + 1) * tq
        q = q_ref[qi * tq:(qi + 1) * tq, :].astype(BF16)
        qpos = qi * tq + lax.broadcasted_iota(jnp.int32, (tq, nk), 0)
        kpos = lax.broadcasted_iota(jnp.int32, (tq, nk), 1)
        ok = kpos <= qpos
        ps = []
        for c in range(2):
            s = lax.dot_general(q[:, c * d:(c + 1) * d], kb[:nk, c * d:(c + 1) * d], nt,
                                preferred_element_type=F32) * scale
            ps.append(_softmax_rows(jnp.where(ok, s, NEG_BIG)))
        attn = ps[0] - lam * ps[1]
        o = jnp.dot(attn.astype(BF16), vb[:nk], preferred_element_type=F32)
        o = o * lax.rsqrt(jnp.mean(o * o, axis=-1, keepdims=True) + eps) * g_ref[...]
        o_ref[qi * tq:(qi + 1) * tq, :] = (o * (1.0 - lambda_init)).astype(o_ref.dtype)


def _diff_prompt(cfg, lp, q, k, v, subln):
    m, wdt = q.shape
    hw = 2 * cfg.diff_head_dim
    s_len = cfg.seq
    blk = pl.BlockSpec((s_len, hw), lambda b, h: (b, h))
    return pl.pallas_call(
        functools.partial(_diff_prompt_kernel, tq=_tile(s_len, 256), d=cfg.diff_head_dim,
                          lambda_init=cfg.lambda_init, eps=cfg.subln_eps),
        out_shape=jax.ShapeDtypeStruct((m, wdt), BF16),
        grid=(m // s_len, wdt // hw),
        in_specs=[pl.BlockSpec(lp.shape, lambda b, h: (0, 0)), blk, blk, blk,
                  pl.BlockSpec((1, hw), lambda b, h: (0, 0))],
        out_specs=blk,
        compiler_params=_params("parallel", "parallel"),
        name="diff_prompt",
    )(lp, q, k, v, subln.reshape(1, hw))


def _diff_scores_kernel(pt_ref, q_ref, *refs, scale):
    del pt_ref
    k_refs, s_ref = refs[:-1], refs[-1]
    q = q_ref[0]
    ps = k_refs[0].shape[1]
    lane = lax.broadcasted_iota(jnp.int32, (q.shape[0], ps), 1)
    for r, k_ref in enumerate(k_refs):
        s = jnp.zeros((q.shape[0], ps), F32)
        for t in range(ps):
            col = jnp.sum(k_ref[0, t] * q, axis=-1, keepdims=True)
            s = jnp.where(lane == t, col, s)
        s_ref[0, :, r * ps:(r + 1) * ps] = s * scale


def _diff_weights_kernel(lp_ref, sa_ref, qa_ref, ka_ref, sb_ref, qb_ref, kb_ref, w_ref, wn_ref, *,
                         scale, lambda_init):
    lam = _diff_lambda(lp_ref[...], lambda_init)
    p, pn = [], []
    for s_r, q_r, k_r in ((sa_ref, qa_ref, ka_ref), (sb_ref, qb_ref, kb_ref)):
        s = s_r[0]
        sn = jnp.sum(q_r[0] * k_r[0], axis=-1, keepdims=True) * scale
        mx = jnp.maximum(jnp.max(s, axis=-1, keepdims=True), sn)
        e = jnp.exp(s - mx)
        en = jnp.exp(sn - mx)
        den = jnp.sum(e, axis=-1, keepdims=True) + en
        p.append(e / den)
        pn.append(en / den)
    w_ref[0] = p[0] - lam * p[1]
    wn_ref[0] = pn[0] - lam * pn[1]


def _diff_pv_kernel(pt_ref, w_ref, wn_ref, vn_ref, rep_ref, subln_ref, *refs, lambda_init, eps):
    del pt_ref
    v_refs, o_ref, acc_ref = refs[:-2], refs[-2], refs[-1]
    pg = pl.program_id(1)

    @pl.when(pg == 0)
    def _():
        acc_ref[...] = wn_ref[0].astype(BF16).astype(F32) * vn_ref[0].astype(BF16).astype(F32)

    _, ps, h, hw = v_refs[0].shape
    row = lax.broadcasted_iota(jnp.int32, (h, ps * h), 0)
    col = lax.broadcasted_iota(jnp.int32, (h, ps * h), 1)
    acc = acc_ref[...]
    for r, v_ref in enumerate(v_refs):
        w = w_ref[0, :, r * ps:(r + 1) * ps].astype(BF16)
        we = jnp.dot(w, rep_ref[...], preferred_element_type=F32)
        wd = jnp.where(col % h == row, we, 0.0).astype(BF16)
        acc = acc + jnp.dot(wd, v_ref[0].reshape(ps * h, hw).astype(BF16), preferred_element_type=F32)
    acc_ref[...] = acc

    @pl.when(pg == pl.num_programs(1) - 1)
    def _():
        o = acc_ref[...]
        ms = jnp.mean(o * o, axis=-1, keepdims=True)
        o_ref[0] = o * lax.rsqrt(ms + eps) * subln_ref[...] * (1.0 - lambda_init)


def _diff_sample(cfg, lp, q, k, v, cache_k, cache_v, page_table, subln):
    db, wdt = q.shape
    h = cfg.diff_heads
    d = cfg.diff_head_dim
    hw = 2 * d
    ps = cfg.page_size
    n_pages = page_table.shape[1]
    n_past = n_pages * ps
    scale = d ** -0.5
    q3, k3 = q.reshape(db, 2 * h, d), k.reshape(db, 2 * h, d)
    v3 = v.reshape(db, h, hw)
    pps = _tile(n_pages, 4)
    rows = lambda nr, nc: pl.BlockSpec((1, nr, nc), lambda b, p, pt: (b, 0, 0))

    def pages(nr, nc):
        return [pl.BlockSpec((1, ps, nr, nc), lambda b, p, pt, r=r: (pt[b, p * pps + r], 0, 0, 0))
                for r in range(pps)]

    s = pl.pallas_call(
        functools.partial(_diff_scores_kernel, scale=scale),
        out_shape=jax.ShapeDtypeStruct((db, 2 * h, n_past), F32),
        grid_spec=pltpu.PrefetchScalarGridSpec(
            num_scalar_prefetch=1, grid=(db, n_pages // pps),
            in_specs=[rows(2 * h, d)] + pages(2 * h, d),
            out_specs=pl.BlockSpec((1, 2 * h, pps * ps), lambda b, p, pt: (b, 0, p))),
        compiler_params=_params("parallel", "arbitrary"),
        name="diff_scores",
    )(page_table, q3, *([cache_k] * pps))
    maps = lambda a: [a.reshape(db, h, 2, a.shape[-1])[:, :, c] for c in range(2)]
    (qa, qb), (ka, kb), (sa, sb) = maps(q3), maps(k3), maps(s)
    headb = pl.BlockSpec((1, h, d), lambda b: (b, 0, 0))
    pastb = pl.BlockSpec((1, h, n_past), lambda b: (b, 0, 0))
    w, wn = pl.pallas_call(
        functools.partial(_diff_weights_kernel, scale=scale, lambda_init=cfg.lambda_init),
        out_shape=(jax.ShapeDtypeStruct((db, h, n_past), F32), jax.ShapeDtypeStruct((db, h, 1), F32)),
        grid=(db,),
        in_specs=[pl.BlockSpec(lp.shape, lambda b: (0, 0)), pastb, headb, headb, pastb, headb, headb],
        out_specs=(pastb, pl.BlockSpec((1, h, 1), lambda b: (b, 0, 0))),
        compiler_params=_params("parallel"),
        name="diff_weights",
    )(lp, sa, qa, ka, sb, qb, kb)
    repeat = (jnp.arange(ps)[:, None] == (jnp.arange(ps * h) // h)[None, :]).astype(BF16)
    o = pl.pallas_call(
        functools.partial(_diff_pv_kernel, lambda_init=cfg.lambda_init, eps=cfg.subln_eps),
        out_shape=jax.ShapeDtypeStruct((db, h, hw), F32),
        grid_spec=pltpu.PrefetchScalarGridSpec(
            num_scalar_prefetch=1, grid=(db, n_pages // pps),
            in_specs=[pl.BlockSpec((1, h, pps * ps), lambda b, p, pt: (b, 0, p)),
                      rows(h, 1), rows(h, hw),
                      pl.BlockSpec((ps, ps * h), lambda b, p, pt: (0, 0)),
                      pl.BlockSpec((1, hw), lambda b, p, pt: (0, 0))] + pages(h, hw),
            out_specs=rows(h, hw),
            scratch_shapes=[pltpu.VMEM((h, hw), F32)]),
        compiler_params=_params("parallel", "arbitrary"),
        name="diff_pv",
    )(page_table, w, wn, v3, repeat, subln.reshape(1, hw), *([cache_v] * pps))
    return o.reshape(db, wdt)


def _top_mask_lanes(gate, n_top):
    nb = gate.shape[1]
    lane = lax.broadcasted_iota(jnp.int32, gate.shape, 1).astype(F32)
    sel = jnp.zeros(gate.shape, F32)
    for _ in range(n_top):
        mx = jnp.max(gate, axis=1, keepdims=True)
        idx = jnp.min(jnp.where(gate == mx, lane, float(nb)), axis=1, keepdims=True)
        hit = (lane == idx) & (mx > 0.5 * NEG_BIG)
        sel = jnp.where(hit, 1.0, sel)
        gate = jnp.where(lane == idx, NEG_BIG, gate)
    return sel


def _moba_prompt_kernel(q_ref, k_ref, v_ref, ext_ref, o_ref, *, bs, n_top):
    s_len, d = q_ref.shape
    nb = s_len // bs
    kf = k_ref[...]
    vb = v_ref[...].astype(BF16)
    kmean = jnp.sum(kf.reshape(nb, bs, d), axis=1) / bs
    kmean = jnp.concatenate([kmean, jnp.zeros((LANES - nb, d), F32)], axis=0)
    k_aug = jnp.concatenate([kf.astype(BF16), ext_ref[...]], axis=1)
    scale = d ** -0.5
    nt = (((1,), (1,)), ((), ()))
    causal = (lax.broadcasted_iota(jnp.int32, (bs, bs), 1) <= lax.broadcasted_iota(jnp.int32, (bs, bs), 0))
    blk = lax.broadcasted_iota(jnp.int32, (bs, LANES), 1)
    for qi in range(nb):
        lo, nk = qi * bs, (qi + 1) * bs
        qf = q_ref[lo:nk, :]
        allowed = blk == qi
        if qi > 0:
            gate = lax.dot_general(qf, kmean, nt, precision=lax.Precision.HIGHEST,
                                   preferred_element_type=F32)
            allowed = allowed | (_top_mask_lanes(jnp.where(blk < qi, gate, NEG_BIG), n_top) > 0.5)
        bias = jnp.where(allowed, 0.0, NEG_BIG).astype(BF16)
        q_aug = jnp.concatenate([(qf * scale).astype(BF16), bias], axis=1)
        s = lax.dot_general(q_aug, k_aug[:nk], nt, preferred_element_type=F32)
        s_own = jnp.where(causal, s[:, lo:nk], NEG_BIG)
        mx = jnp.max(s_own, axis=-1, keepdims=True)
        if qi > 0:
            mx = jnp.maximum(mx, jnp.max(s[:, :lo], axis=-1, keepdims=True))
        e_own = jnp.exp(s_own - mx)
        den = jnp.sum(e_own, axis=-1, keepdims=True)
        o = jnp.dot(e_own.astype(BF16), vb[lo:nk], preferred_element_type=F32)
        if qi > 0:
            e_past = jnp.exp(s[:, :lo] - mx)
            den = den + jnp.sum(e_past, axis=-1, keepdims=True)
            o = o + jnp.dot(e_past.astype(BF16), vb[:lo], preferred_element_type=F32)
        o_ref[lo:nk, :] = (o / den).astype(o_ref.dtype)


def _moba_prompt(cfg, q, k, v):
    m, wdt = q.shape
    d = cfg.moba_head_dim
    s_len = cfg.seq
    bs = cfg.moba_block
    nb = s_len // bs
    assert nb <= LANES
    onehot = ((jnp.arange(s_len) // bs)[:, None] == jnp.arange(LANES)[None, :]).astype(BF16)
    blk = pl.BlockSpec((s_len, d), lambda b, h: (b, h))
    return pl.pallas_call(
        functools.partial(_moba_prompt_kernel, bs=bs, n_top=min(cfg.moba_topk, nb)),
        out_shape=jax.ShapeDtypeStruct((m, wdt), BF16),
        grid=(m // s_len, wdt // d),
        in_specs=[blk, blk, blk, pl.BlockSpec((s_len, LANES), lambda b, h: (0, 0))],
        out_specs=blk,
        compiler_params=_params("parallel", "parallel"),
        name="moba_prompt",
    )(q, k, v, onehot)


def _moba_select_kernel(pt_ref, q_ref, kn_ref, *refs, ppb, bs, n_past_blocks, own_block, n_top):
    del pt_ref
    k_refs, sel_ref, gate_ref = refs[:ppb], refs[ppb], refs[ppb + 1]
    blk = pl.program_id(1)
    q = q_ref[0]
    lane = lax.broadcasted_iota(jnp.int32, gate_ref.shape, 1)

    @pl.when(blk == 0)
    def _():
        gate_ref[...] = jnp.zeros(gate_ref.shape, F32)

    def body(t, tot):
        for r in k_refs:
            tot = tot + r[0, t]
        return tot

    tot = lax.fori_loop(0, k_refs[0].shape[1], body, jnp.zeros(q.shape, F32), unroll=8)
    g = jnp.sum(tot * q, axis=-1, keepdims=True) / bs
    gate_ref[...] = jnp.where(lane == blk, g, gate_ref[...])

    @pl.when(blk == pl.num_programs(1) - 1)
    def _():
        g_own = jnp.sum(kn_ref[0] * q, axis=-1, keepdims=True) / bs
        gate = gate_ref[...] + jnp.where(lane == own_block, g_own, 0.0)
        gate = jnp.where((lane < own_block) & (lane < n_past_blocks), gate, NEG_BIG)
        lanef = lane.astype(F32)
        out = jnp.zeros(gate.shape, F32)
        for j in range(n_top):
            mx = jnp.max(gate, axis=1, keepdims=True)
            idx = jnp.min(jnp.where(gate == mx, lanef, float(LANES)), axis=1, keepdims=True)
            out = jnp.where(lane == j, idx, out)
            gate = jnp.where(lanef == idx, NEG_BIG, gate)
        sel_ref[0] = out.astype(jnp.int32)


def _moba_sample_attn_kernel(sel_ref, pt_ref, q_ref, kn_ref, vn_ref, *refs, n_pg, scale, hg):
    del sel_ref, pt_ref
    k_refs, v_refs, o_ref = refs[:n_pg], refs[n_pg:2 * n_pg], refs[2 * n_pg]
    hh = pl.program_id(1) % hg
    ps = k_refs[0].shape[1]
    q = q_ref[0, 0]
    row = lax.broadcasted_iota(jnp.int32, (hg, ps), 0)
    lane = lax.broadcasted_iota(jnp.int32, (hg, ps), 1)

    def page_scores(r):
        s8 = jnp.zeros((hg, ps), F32)
        for t in range(ps):
            col = jnp.sum(r[0, t] * q, axis=-1, keepdims=True)
            s8 = jnp.where(lane == t, col, s8)
        return jnp.sum(jnp.where(row == hh, s8, 0.0), axis=0, keepdims=True) * scale

    s = [page_scores(r) for r in k_refs]
    sn = jnp.sum(q * kn_ref[0, 0], axis=-1, keepdims=True) * scale
    mx = sn
    for x in s:
        mx = jnp.maximum(mx, jnp.max(x, axis=-1, keepdims=True))
    e = [jnp.exp(x - mx) for x in s]
    en = jnp.exp(sn - mx)
    den = en
    for x in e:
        den = den + jnp.sum(x, axis=-1, keepdims=True)
    n_acc = 4
    accs = [jnp.zeros((hg, q.shape[1]), F32)] * n_acc
    for x, r in zip(e, v_refs):
        for t in range(ps):
            accs[t % n_acc] = accs[t % n_acc] + r[0, t] * x[:, t:t + 1]
    acc = (accs[0] + accs[1]) + (accs[2] + accs[3])
    row_d = lax.broadcasted_iota(jnp.int32, acc.shape, 0)
    o = jnp.sum(jnp.where(row_d == hh, acc, 0.0), axis=0, keepdims=True) + en * vn_ref[0, 0]
    o_ref[0, 0] = o / den


def _moba_sample(cfg, q, k, v, cache_k, cache_v, page_table):
    db, wdt = q.shape
    d = cfg.moba_head_dim
    nh = cfg.moba_heads
    ps, bs = cfg.page_size, cfg.moba_block
    ppb = bs // ps
    n_pages = page_table.shape[1]
    n_pool = cache_k.shape[0]
    assert n_pages % ppb == 0 and cfg.past_len == n_pages * ps
    nbp = n_pages // ppb
    own = cfg.past_len // bs
    n_top = cfg.moba_topk
    assert nbp >= n_top and own == nbp and own < LANES
    hg = min(SUBLANES, nh)
    assert nh % hg == 0
    q3, k3 = q.reshape(db, nh, d), k.reshape(db, nh, d)

    def page_spec(r):
        return pl.BlockSpec((1, ps, nh, d), lambda b, n, pt: (pt[b, n * ppb + r], 0, 0, 0))

    allh = pl.BlockSpec((1, nh, d), lambda b, n, pt: (b, 0, 0))
    sel = pl.pallas_call(
        functools.partial(_moba_select_kernel, ppb=ppb, bs=float(bs), n_past_blocks=nbp, own_block=own,
                          n_top=n_top),
        out_shape=jax.ShapeDtypeStruct((db, nh, LANES), jnp.int32),
        grid_spec=pltpu.PrefetchScalarGridSpec(
            num_scalar_prefetch=1, grid=(db, nbp),
            in_specs=[allh, allh] + [page_spec(r) for r in range(ppb)],
            out_specs=pl.BlockSpec((1, nh, LANES), lambda b, n, pt: (b, 0, 0)),
            scratch_shapes=[pltpu.VMEM((nh, LANES), F32)]),
        compiler_params=_params("parallel", "arbitrary"),
        name="moba_select",
    )(page_table, q3, k3, *([cache_k] * ppb))
    sel_flat = sel[:, :, :n_top].reshape(-1)
    pt_flat = page_table.reshape(-1)
    n_pg = n_top * ppb

    def sel_page(jp):
        j, r = divmod(jp, ppb)
        return pl.BlockSpec(
            (1, ps, hg, d),
            lambda b, h, sl, pt: (pt[b * n_pages + sl[(b * nh + h) * n_top + j] * ppb + r], 0, h // hg, 0))

    q4, k4, v4 = (a.reshape(db, nh, 1, d) for a in (q, k, v))
    headb = pl.BlockSpec((1, 1, 1, d), lambda b, h, sl, pt: (b, h, 0, 0))
    o = pl.pallas_call(
        functools.partial(_moba_sample_attn_kernel, n_pg=n_pg, scale=d ** -0.5, hg=hg),
        out_shape=jax.ShapeDtypeStruct((db, nh, 1, d), F32),
        grid_spec=pltpu.PrefetchScalarGridSpec(
            num_scalar_prefetch=2, grid=(db, nh),
            in_specs=[headb, headb, headb] + [sel_page(jp) for jp in range(n_pg)] * 2,
            out_specs=headb),
        compiler_params=_params("parallel", "parallel"),
        name="moba_sample_attn",
    )(sel_flat, pt_flat, q4, k4, v4, *([cache_k] * n_pg), *([cache_v] * n_pg))
    return o.reshape(db, wdt)


def _rope_tables(cfg, pos, d):
    half = d // 2
    inv = cfg.rope_theta ** (-jnp.arange(half, dtype=F32) / half)
    ang = pos.astype(F32)[:, None] * inv[None, :]
    cos, sin = jnp.cos(ang), jnp.sin(ang)
    return jnp.concatenate([cos, cos], axis=-1), jnp.concatenate([-sin, sin], axis=-1)


def _rwkv_layout(cfg):
    c = cfg.rwkv_width
    segs, o = [], 0
    for wd in (c, c, c, cfg.lora_w, cfg.lora_a, cfg.lora_g):
        segs.append((o, wd, _rup(wd, LANES)))
        o += wd
    return segs, o


def _pad_cols(cfg, a):
    segs, _ = _rwkv_layout(cfg)
    parts = []
    for o, wd, pw in segs:
        x = a[..., o:o + wd]
        if pw != wd:
            x = jnp.pad(x, [(0, 0)] * (a.ndim - 1) + [(0, pw - wd)])
        parts.append(x)
    return jnp.concatenate(parts, axis=-1)


def _unpad_cols(cfg, a):
    segs, _ = _rwkv_layout(cfg)
    parts, o = [], 0
    for _, wd, pw in segs:
        parts.append(a[..., o:o + wd])
        o += pw
    return jnp.concatenate(parts, axis=-1)


def _pad_rows(a, rows):
    return jnp.pad(a, ((0, rows - a.shape[0]), (0, 0)))


def _state_to_pairs(cfg, s):
    b, h, n, _ = s.shape
    return jnp.transpose(s.reshape(b, h // 2, 2, n, n), (0, 1, 3, 2, 4)).reshape(b, h // 2, n, 2 * n)


def _pairs_to_state(cfg, s):
    b, hp, n, _ = s.shape
    return jnp.transpose(s.reshape(b, hp, n, 2, n), (0, 1, 3, 2, 4)).reshape(b, 2 * hp, n, n)


def _forward(cfg, x_prompt, x_sample, cache_k0, cache_v0, cache_k1, cache_v1, page_table,
             state_wkv0, state_shift0, state_conv,
             norm_mix, norm_ffn, norm_final, w_in0, w_out0,
             rwkv_mu, rwkv_w0, rwkv_w2, rwkv_a0, rwkv_a2, rwkv_g2, rwkv_kk, rwkv_ka, rwkv_rk,
             rwkv_ln_w, rwkv_ln_b, diff_lq1, diff_lk1, diff_lq2, diff_lk2, diff_subln,
             w_in1, w_out1, ffn_w_up, ffn_conv_w, ffn_conv_b, ffn_w_down):
    bsz, s_len, dm = x_prompt.shape
    db = x_sample.shape[0]
    c = cfg.rwkv_width
    dw = dm - c
    segs, rproj = _rwkv_layout(cfg)
    n_heads = c // cfg.rwkv_head_dim

    w_r = _pad_cols(cfg, w_in0[:, :rproj]).astype(BF16)
    w_q, w_k, w_v = (w_in0[:, rproj + i * dw:rproj + (i + 1) * dw].astype(BF16) for i in range(3))
    w_out0_b, w_in1_b, w_out1_b, w_up_b = w_out0, w_in1, w_out1, ffn_w_up
    w_down_b = ffn_w_down.astype(BF16)
    lw_p, la_p = _rup(cfg.lora_w, LANES), _rup(cfg.lora_a, LANES)
    row = lambda a: a.reshape(1, -1)
    prm = dict(mu=row(_pad_cols(cfg, rwkv_mu)), w0=row(rwkv_w0), a0=row(rwkv_a0), kk=row(rwkv_kk),
               ka=row(rwkv_ka), w2=_pad_rows(rwkv_w2, lw_p).astype(BF16),
               a2=_pad_rows(rwkv_a2, la_p).astype(BF16),
               g2=_pad_rows(rwkv_g2, _rup(cfg.lora_g, LANES)).astype(BF16),
               ln_w=row(rwkv_ln_w), ln_b=row(rwkv_ln_b), rk=row(rwkv_rk))
    lp = jnp.stack([diff_lq1, diff_lk1, diff_lq2, diff_lk2])
    rope_p = _rope_tables(cfg, jnp.arange(s_len, dtype=jnp.int32), cfg.diff_head_dim)
    rope_s = _rope_tables(cfg, jnp.full((db,), cfg.past_len, jnp.int32), cfg.diff_head_dim)

    xp = x_prompt.reshape(bsz * s_len, dm)
    xs = x_sample.reshape(db, dm)
    out = {}
    conv_p, conv_s = [], []
    for layer in range(2):
        hp = _rmsnorm(xp, norm_mix[layer], cfg.norm_eps, BF16)
        hs = _rmsnorm(xs, norm_mix[layer], cfg.norm_eps, BF16)
        if layer == 0:
            mix = {}
            for tag, h, n_seq, t_len, rope in (("p", hp, bsz, s_len, rope_p), ("s", hs, db, 1, rope_s)):
                rows = n_seq * t_len
                p = _matmul(h, w_r)
                if tag == "p":
                    p3 = p.reshape(n_seq, t_len, -1)
                    p_prev = jnp.concatenate([jnp.zeros_like(p3[:, :1]), p3[:, :-1]], axis=1).reshape(rows, -1)
                    s0 = jnp.zeros((n_seq, n_heads // 2, cfg.rwkv_head_dim, LANES), F32)
                    shift = p3[:, -1]
                else:
                    p_prev = _pad_cols(cfg, state_shift0)
                    s0 = _state_to_pairs(cfg, state_wkv0)
                    shift = p
                r, w, k, v, kk, b, g = _rwkv_prep(cfg, p, p_prev, prm)
                y, sf = _rwkv_scan(cfg, r, w, k, v, kk, b, s0, n_seq, t_len)
                a_out = _rwkv_post(cfg, y, r, k, v, g, prm)
                q0 = _matmul(h, w_q, rope=rope, rows_per_seq=t_len if tag == "p" else db)
                k0 = _matmul(h, w_k, rope=rope, rows_per_seq=t_len if tag == "p" else db)
                v0 = _matmul(h, w_v)
                if tag == "p":
                    b_out = _diff_prompt(cfg, lp, q0, k0, v0, diff_subln)
                else:
                    b_out = _diff_sample(cfg, lp, q0, k0, v0, cache_k0, cache_v0, page_table,
                                         diff_subln).astype(BF16)
                mix[tag] = (a_out, b_out)
                out["k0" + tag] = k0.reshape(n_seq, t_len, 2 * cfg.diff_heads, cfg.diff_head_dim)
                out["v0" + tag] = v0.reshape(n_seq, t_len, cfg.diff_heads, 2 * cfg.diff_head_dim)
                out["wkv" + tag] = _pairs_to_state(cfg, sf)
                out["shift" + tag] = _unpad_cols(cfg, shift)
            xp = _matmul(mix["p"][0], w_out0_b, x2=mix["p"][1], w2=w_out0_b, w2_row0=c, res=xp)
            xs = _matmul(mix["s"][0], w_out0_b, x2=mix["s"][1], w2=w_out0_b, w2_row0=c, res=xs)
        else:
            qkv = []
            for tag, h, n_seq, t_len, rope in (("p", hp, bsz, s_len, rope_p), ("s", hs, db, 1, rope_s)):
                rps = t_len if tag == "p" else db
                q1 = _matmul(h, w_in1_b, cols=(0, dm), rope=rope, rows_per_seq=rps)
                k1 = _matmul(h, w_in1_b, cols=(dm, dm), rope=rope, rows_per_seq=rps)
                v1 = _matmul(h, w_in1_b, cols=(2 * dm, dm))
                out["k1" + tag] = k1.reshape(n_seq, t_len, cfg.moba_heads, cfg.moba_head_dim)
                out["v1" + tag] = v1.reshape(n_seq, t_len, cfg.moba_heads, cfg.moba_head_dim)
                qkv.append((q1, k1, v1))
            mp = _moba_prompt(cfg, *qkv[0])
            ms = _moba_sample(cfg, *qkv[1], cache_k1, cache_v1, page_table).astype(BF16)
            xp = _matmul(mp, w_out1_b, res=xp)
            xs = _matmul(ms, w_out1_b, res=xs)
        hp = _rmsnorm(xp, norm_ffn[layer], cfg.norm_eps, BF16)
        hs = _rmsnorm(xs, norm_ffn[layer], cfg.norm_eps, BF16)
        fp, cp = _ffn_up(hp, w_up_b, layer, ffn_conv_w[layer], ffn_conv_b[layer], s_len)
        prev = state_conv[layer]
        fs, gs = _ffn_up_step(hs, w_up_b, layer, ffn_conv_w[layer], ffn_conv_b[layer], prev[:, 0], prev[:, 1])
        xp = _matmul(fp, w_down_b, w_lead=layer, res=xp, tm=512, tn=256)
        xs = _matmul(fs, w_down_b, w_lead=layer, res=xs, tm=512, tn=256)
        conv_p.append(cp)
        conv_s.append(jnp.stack([prev[:, 1], gs], axis=1))
    y_prompt = _rmsnorm(xp, norm_final, cfg.norm_eps, F32).reshape(bsz, s_len, dm)
    y_sample = _rmsnorm(xs, norm_final, cfg.norm_eps, F32).reshape(db, 1, dm)
    return (y_prompt, y_sample, out["k0p"], out["v0p"], out["k0s"], out["v0s"],
            out["wkvp"], out["wkvs"], out["shiftp"], out["shifts"],
            out["k1p"], out["v1p"], out["k1s"], out["v1s"], jnp.stack(conv_p), jnp.stack(conv_s))


def kernel(x_prompt, x_sample, cache_k0, cache_v0, cache_k1, cache_v1, page_table, state_wkv0, state_shift0, state_conv, norm_mix, norm_ffn, norm_final, w_in0, w_out0, rwkv_mu, rwkv_w0, rwkv_w2, rwkv_a0, rwkv_a2, rwkv_g2, rwkv_kk, rwkv_ka, rwkv_rk, rwkv_ln_w, rwkv_ln_b, diff_lq1, diff_lk1, diff_lq2, diff_lk2, diff_subln, w_in1, w_out1, ffn_w_up, ffn_conv_w, ffn_conv_b, ffn_w_down):
    return _forward(_CFG, x_prompt, x_sample, cache_k0, cache_v0, cache_k1, cache_v1, page_table,
                    state_wkv0, state_shift0, state_conv, norm_mix, norm_ffn, norm_final, w_in0, w_out0,
                    rwkv_mu, rwkv_w0, rwkv_w2, rwkv_a0, rwkv_a2, rwkv_g2, rwkv_kk, rwkv_ka, rwkv_rk,
                    rwkv_ln_w, rwkv_ln_b, diff_lq1, diff_lk1, diff_lq2, diff_lk2, diff_subln,
                    w_in1, w_out1, ffn_w_up, ffn_conv_w, ffn_conv_b, ffn_w_down)
```

```python
import functools
import math
from typing import NamedTuple

import jax
import jax.numpy as jnp
from jax import lax
from jax.experimental import pallas as pl
from jax.experimental.pallas import tpu as pltpu

F32 = jnp.float32
BF16 = jnp.bfloat16
LANES = 128
SUBLANES = 8
NEG_BIG = -1e30
VMEM_LIMIT = 56 * 1024 * 1024


class Cfg(NamedTuple):
    d_model: int
    batch: int
    seq: int
    dec_batch: int
    past_len: int
    page_size: int
    rwkv_width: int
    rwkv_head_dim: int
    lora_w: int
    lora_a: int
    lora_g: int
    diff_heads: int
    diff_head_dim: int
    moba_heads: int
    moba_head_dim: int
    moba_block: int
    moba_topk: int
    d_ff: int
    rope_theta: float
    norm_eps: float
    subln_eps: float
    gn_eps: float
    lambda_init: float


def _make_cfg(d_model=4096, batch=4, seq=2048, dec_batch=8, past_len=16384, page_size=128, d_ff=11008,
              diff_heads=8, moba_block=256):
    rw = d_model // 2
    lw = max(32, int(round(1.8 * rw ** 0.5 / 32)) * 32)
    lg = max(32, int(round(0.6 * rw ** 0.8 / 32)) * 32)
    return Cfg(d_model=d_model, batch=batch, seq=seq, dec_batch=dec_batch, past_len=past_len,
               page_size=page_size, rwkv_width=rw, rwkv_head_dim=64, lora_w=lw, lora_a=lw, lora_g=lg,
               diff_heads=diff_heads, diff_head_dim=(d_model - rw) // (2 * diff_heads),
               moba_heads=d_model // 128, moba_head_dim=128, moba_block=moba_block, moba_topk=3,
               d_ff=d_ff, rope_theta=10000.0, norm_eps=1e-6, subln_eps=1e-5, gn_eps=64e-5,
               lambda_init=0.8 - 0.6 * math.exp(-0.3 * 0))


_CFG = _make_cfg()


def _rup(n, m):
    return -(-n // m) * m


def _tile(n, pref):
    if n <= pref:
        return n
    t = pref
    while n % t:
        t -= SUBLANES
    return t


def _params(*sem):
    return pltpu.CompilerParams(dimension_semantics=sem, vmem_limit_bytes=VMEM_LIMIT)


def _split_bf16(x):
    hi = x.astype(BF16)
    lo = (x - hi.astype(F32)).astype(BF16)
    return hi, lo


def _seg_dot(x, ind2):
    hi, lo = _split_bf16(x)
    return jnp.dot(jnp.concatenate([hi, lo], axis=1), ind2, preferred_element_type=F32)


def _rms_kernel(x_ref, g_ref, o_ref, *, eps):
    x = x_ref[...]
    ms = jnp.mean(x * x, axis=-1, keepdims=True)
    o_ref[...] = (x * lax.rsqrt(ms + eps) * g_ref[...]).astype(o_ref.dtype)


def _rmsnorm(x, g, eps, out_dtype):
    m, d = x.shape
    tm = _tile(m, 256)
    return pl.pallas_call(
        functools.partial(_rms_kernel, eps=eps),
        out_shape=jax.ShapeDtypeStruct((m, d), out_dtype),
        grid=(m // tm,),
        in_specs=[pl.BlockSpec((tm, d), lambda i: (i, 0)), pl.BlockSpec((1, d), lambda i: (0, 0))],
        out_specs=pl.BlockSpec((tm, d), lambda i: (i, 0)),
        compiler_params=_params("parallel"),
        name="rmsnorm",
    )(x, g.reshape(1, d))


def _mm_kernel(*refs, has_res, has_rope, two_lhs):
    it = iter(refs)
    x_ref = next(it)
    x2_ref = next(it) if two_lhs else None
    w_ref = next(it)
    w2_ref = next(it) if two_lhs else None
    res_ref = next(it) if has_res else None
    cos_ref = next(it) if has_rope else None
    sin_ref = next(it) if has_rope else None
    o_ref = next(it)
    acc = jnp.dot(x_ref[...], w_ref[...].astype(BF16), preferred_element_type=F32)
    if two_lhs:
        acc = acc + jnp.dot(x2_ref[...], w2_ref[...].astype(BF16), preferred_element_type=F32)
    if has_rope:
        cos = cos_ref[...]
        sin = sin_ref[...]
        for s in range(acc.shape[1] // LANES):
            a = acc[:, s * LANES:(s + 1) * LANES]
            o_ref[:, s * LANES:(s + 1) * LANES] = (
                a * cos + pltpu.roll(a, LANES // 2, axis=1) * sin).astype(o_ref.dtype)
    else:
        if has_res:
            acc = acc + res_ref[...]
        o_ref[...] = acc.astype(o_ref.dtype)


def _matmul(x, w, *, w_lead=None, cols=None, x2=None, w2=None, w2_row0=0, res=None, rope=None,
            rows_per_seq=None, out_dtype=F32, tm=1024, tn=512):
    m, k = x.shape
    col0, n = cols if cols is not None else (0, w.shape[-1])
    tm = _tile(m if rope is None else rows_per_seq, tm)
    tn = _tile(n, tn)
    assert col0 % tn == 0
    jb = col0 // tn
    two = x2 is not None
    in_specs = [pl.BlockSpec((tm, k), lambda i, j: (i, 0))]
    args = [x]
    if two:
        in_specs.append(pl.BlockSpec((tm, x2.shape[1]), lambda i, j: (i, 0)))
        args.append(x2)
    if w.ndim == 3:
        in_specs.append(pl.BlockSpec((None, k, tn), lambda i, j: (w_lead, 0, jb + j)))
    else:
        in_specs.append(pl.BlockSpec((k, tn), lambda i, j: (0, jb + j)))
    args.append(w)
    if two:
        k2 = x2.shape[1]
        assert w2_row0 % k2 == 0
        in_specs.append(pl.BlockSpec((k2, tn), lambda i, j: (w2_row0 // k2, jb + j)))
        args.append(w2)
    if res is not None:
        in_specs.append(pl.BlockSpec((tm, tn), lambda i, j: (i, j)))
        args.append(res)
    if rope is not None:
        tps = rows_per_seq // tm
        for t in rope:
            in_specs.append(pl.BlockSpec((tm, LANES), lambda i, j: (i % tps, 0)))
            args.append(t)
    return pl.pallas_call(
        functools.partial(_mm_kernel, has_res=res is not None, has_rope=rope is not None, two_lhs=two),
        out_shape=jax.ShapeDtypeStruct((m, n), out_dtype),
        grid=(m // tm, n // tn),
        in_specs=in_specs,
        out_specs=pl.BlockSpec((tm, tn), lambda i, j: (i, j)),
        compiler_params=_params("parallel", "parallel"),
        name="matmul",
    )(*args)


def _ffn_up_kernel(x_ref, wg_ref, wu_ref, cw_ref, cb_ref, h_ref, cs_ref, carry_ref, *, tiles_per_seq):
    i = pl.program_id(0)
    j = pl.program_id(1)
    x = x_ref[...]
    g = jnp.dot(x, wg_ref[...].astype(BF16), preferred_element_type=F32)
    u = jnp.dot(x, wu_ref[...].astype(BF16), preferred_element_type=F32)
    tm = g.shape[0]

    @pl.when(i % tiles_per_seq == 0)
    def _():
        carry_ref[j] = jnp.zeros(carry_ref.shape[1:], F32)

    carry = carry_ref[j]
    row = lax.broadcasted_iota(jnp.int32, g.shape, 0)
    g1 = jnp.where(row == 0, carry[1:2], pltpu.roll(g, 1, axis=0))
    g2 = jnp.where(row == 0, carry[0:1], jnp.where(row == 1, carry[1:2], pltpu.roll(g, 2, axis=0)))
    cw = cw_ref[...]
    gc = cb_ref[...] + cw[0:1] * g2
    gc = gc + cw[1:2] * g1
    gc = gc + cw[2:3] * g
    h_ref[...] = (gc * jax.nn.sigmoid(gc) * u).astype(h_ref.dtype)
    last2 = g[tm - 2:tm, :]
    carry_ref[j, 0:2, :] = last2
    cs_ref[0] = last2


def _ffn_up(x, w_up, layer, conv_w, conv_b, rows_per_seq, tm=1024, tn=256):
    m, d = x.shape
    f = w_up.shape[2] // 2
    tm = _tile(rows_per_seq, tm)
    tn = _tile(f, tn)
    nj = f // tn
    tps = rows_per_seq // tm
    h, tails = pl.pallas_call(
        functools.partial(_ffn_up_kernel, tiles_per_seq=tps),
        out_shape=(jax.ShapeDtypeStruct((m, f), BF16),
                   jax.ShapeDtypeStruct((m // tm, 2, f), F32)),
        grid=(m // tm, nj),
        in_specs=[pl.BlockSpec((tm, d), lambda i, j: (i, 0)),
                  pl.BlockSpec((None, d, tn), lambda i, j: (layer, 0, j)),
                  pl.BlockSpec((None, d, tn), lambda i, j: (layer, 0, nj + j)),
                  pl.BlockSpec((3, tn), lambda i, j: (0, j)),
                  pl.BlockSpec((1, tn), lambda i, j: (0, j))],
        out_specs=(pl.BlockSpec((tm, tn), lambda i, j: (i, j)),
                   pl.BlockSpec((1, 2, tn), lambda i, j: (i, 0, j))),
        scratch_shapes=[pltpu.VMEM((nj, SUBLANES, tn), F32)],
        compiler_params=_params("arbitrary", "arbitrary"),
        name="ffn_up",
    )(x, w_up, w_up, conv_w, conv_b.reshape(1, f))
    return h, tails[tps - 1::tps]


def _ffn_up_step_kernel(x_ref, wg_ref, wu_ref, cw_ref, cb_ref, p0_ref, p1_ref, h_ref, g_ref):
    x = x_ref[...]
    g = jnp.dot(x, wg_ref[...].astype(BF16), preferred_element_type=F32)
    u = jnp.dot(x, wu_ref[...].astype(BF16), preferred_element_type=F32)
    cw = cw_ref[...]
    gc = cb_ref[...] + cw[0:1] * p0_ref[...]
    gc = gc + cw[1:2] * p1_ref[...]
    gc = gc + cw[2:3] * g
    h_ref[...] = (gc * jax.nn.sigmoid(gc) * u).astype(h_ref.dtype)
    g_ref[...] = g


def _ffn_up_step(x, w_up, layer, conv_w, conv_b, prev0, prev1, tn=256):
    m, d = x.shape
    f = w_up.shape[2] // 2
    tn = _tile(f, tn)
    nj = f // tn
    row = lambda j: (0, j)
    return pl.pallas_call(
        _ffn_up_step_kernel,
        out_shape=(jax.ShapeDtypeStruct((m, f), BF16), jax.ShapeDtypeStruct((m, f), F32)),
        grid=(nj,),
        in_specs=[pl.BlockSpec((m, d), lambda j: (0, 0)),
                  pl.BlockSpec((None, d, tn), lambda j: (layer, 0, j)),
                  pl.BlockSpec((None, d, tn), lambda j: (layer, 0, nj + j)),
                  pl.BlockSpec((3, tn), row),
                  pl.BlockSpec((1, tn), row),
                  pl.BlockSpec((m, tn), row),
                  pl.BlockSpec((m, tn), row)],
        out_specs=(pl.BlockSpec((m, tn), row), pl.BlockSpec((m, tn), row)),
        compiler_params=_params("parallel"),
        name="ffn_up_step",
    )(x, w_up, w_up, conv_w, conv_b.reshape(1, f), prev0, prev1)


def _softplus(x):
    return jnp.maximum(x, 0.0) + jnp.log(1.0 + jnp.exp(-jnp.abs(x)))


def _rwkv_prep_kernel(p_ref, pp_ref, mu_ref, w0_ref, a0_ref, kk_ref, ka_ref, w2_ref, a2_ref, g2_ref, bd_ref,
                      r_ref, w_ref, k_ref, v_ref, kk_o_ref, b_ref, g_ref, *, c, lw, la, tiles_per_seq):
    p = p_ref[...]
    if tiles_per_seq is None:
        pp = pp_ref[...]
    else:
        first = pl.program_id(0) % tiles_per_seq == 0
        prev_row = jnp.where(first, 0.0, pp_ref[SUBLANES - 1:SUBLANES, :])
        rowid = lax.broadcasted_iota(jnp.int32, p.shape, 0)
        pp = jnp.where(rowid == 0, prev_row, pltpu.roll(p, 1, axis=0))
    xm = p + (pp - p) * mu_ref[...]
    r = xm[:, 0:c]
    k = xm[:, c:2 * c]
    v = xm[:, 2 * c:3 * c]
    o = 3 * c
    wl = xm[:, o:o + lw]
    al = xm[:, o + lw:o + lw + la]
    gl = xm[:, o + lw + la:]
    z = w0_ref[...] + jnp.dot(jnp.tanh(wl).astype(BF16), w2_ref[...], preferred_element_type=F32)
    w = -_softplus(-z) - 0.5
    a = jax.nn.sigmoid(a0_ref[...] + jnp.dot(al.astype(BF16), a2_ref[...], preferred_element_type=F32))
    g = jnp.dot(jax.nn.sigmoid(gl).astype(BF16), g2_ref[...], preferred_element_type=F32)
    kk = k * kk_ref[...]
    decay = jnp.exp(-jnp.exp(w))
    k2 = k * (1.0 + (a - 1.0) * ka_ref[...])
    bd2 = bd_ref[...]
    for s in range(c // LANES):
        sl = slice(s * LANES, (s + 1) * LANES)
        kks = kk[:, sl]
        ss = _seg_dot(kks * kks, bd2)
        kkn = kks / jnp.maximum(jnp.sqrt(ss), 1e-12)
        kk_o_ref[s] = kkn
        b_ref[s] = kkn * a[:, sl]
        r_ref[s] = r[:, sl]
        w_ref[s] = decay[:, sl]
        k_ref[s] = k2[:, sl]
        v_ref[s] = v[:, sl]
        g_ref[s] = g[:, sl]


def _head_pair_indicator(n):
    i = jnp.arange(LANES)
    bd = (i[:, None] // n == i[None, :] // n).astype(BF16)
    return jnp.concatenate([bd, bd], axis=0)


def _rwkv_prep(cfg, p, p_prev, prm, rows_per_seq=None):
    m = p.shape[0]
    c = cfg.rwkv_width
    pw = p.shape[1]
    lw, la = _rup(cfg.lora_w, LANES), _rup(cfg.lora_a, LANES)
    lg = pw - 3 * c - lw - la
    tm = _tile(m if p_prev is not None else rows_per_seq, 128)
    full = lambda i: (0, 0)
    rowblk = lambda i: (i, 0)
    out = jax.ShapeDtypeStruct((c // LANES, m, LANES), F32)
    if p_prev is None:
        assert tm % SUBLANES == 0
        prev_spec = pl.BlockSpec((SUBLANES, pw), lambda i: (jnp.maximum(i * (tm // SUBLANES) - 1, 0), 0))
        p_prev, tiles_per_seq = p, rows_per_seq // tm
    else:
        prev_spec, tiles_per_seq = pl.BlockSpec((tm, pw), rowblk), None
    return pl.pallas_call(
        functools.partial(_rwkv_prep_kernel, c=c, lw=lw, la=la, tiles_per_seq=tiles_per_seq),
        out_shape=(out,) * 7,
        grid=(m // tm,),
        in_specs=[pl.BlockSpec((tm, pw), rowblk), prev_spec,
                  pl.BlockSpec((1, pw), full), pl.BlockSpec((1, c), full), pl.BlockSpec((1, c), full),
                  pl.BlockSpec((1, c), full), pl.BlockSpec((1, c), full),
                  pl.BlockSpec((lw, c), full), pl.BlockSpec((la, c), full), pl.BlockSpec((lg, c), full),
                  pl.BlockSpec((2 * LANES, LANES), full)],
        out_specs=(pl.BlockSpec((c // LANES, tm, LANES), lambda i: (0, i, 0)),) * 7,
        compiler_params=_params("parallel"),
        name="rwkv_prep",
    )(p, p_prev, prm["mu"], prm["w0"], prm["a0"], prm["kk"], prm["ka"], prm["w2"], prm["a2"], prm["g2"],
      _head_pair_indicator(cfg.rwkv_head_dim))


def _rwkv_scan_kernel(r_ref, w_ref, k_ref, v_ref, kk_ref, b_ref, s0_ref, bd_ref,
                      y_ref, sf_ref, ysc_ref, vt_ref, *st_refs, n_pairs, tc):
    t_blk = pl.program_id(2)

    @pl.when(t_blk == 0)
    def _():
        for p in range(n_pairs):
            st_refs[p][...] = s0_ref[0, p]

    rows = min(tc, SUBLANES)
    n = st_refs[0].shape[0]

    nt = (((1,), (1,)), ((), ()))
    row8 = lax.broadcasted_iota(jnp.int32, (SUBLANES, LANES), 0)
    lane8 = lax.broadcasted_iota(jnp.int32, (SUBLANES, LANES), 1)
    rmask = ((lane8 // n == row8 % 2) & (row8 < 4)).astype(F32)
    lane_n = lax.broadcasted_iota(jnp.int32, (n, LANES), 1)

    def group(t0):
        vs = [v_ref[p, pl.ds(t0, rows), :] for p in range(n_pairs)]
        pad = LANES - n_pairs * rows
        vt_ref[...] = jnp.transpose(
            jnp.concatenate(vs + ([jnp.zeros((pad, LANES), F32)] if pad else []), axis=0))
        for i in range(rows):
            for p in range(n_pairs):
                row = lambda ref: ref[p, pl.ds(t0 + i, 1), :]
                s = st_refs[p][...]
                z = _seg_dot(s * row(kk_ref), bd_ref[...])
                c = p * rows + i
                vb = jnp.where(lane_n < n, jnp.broadcast_to(vt_ref[0:n, c:c + 1], (n, LANES)),
                               jnp.broadcast_to(vt_ref[n:2 * n, c:c + 1], (n, LANES)))
                s = s * row(w_ref) - z * row(b_ref) + vb * row(k_ref)
                st_refs[p][...] = s
                r = row(r_ref)
                r_hi = r.astype(BF16).astype(F32)
                rm = (jnp.where(row8 < 2, r_hi, r - r_hi) * rmask).astype(BF16)
                y8 = lax.dot_general(rm, s.astype(BF16), nt, preferred_element_type=F32)
                y8 = y8[0:2] + y8[2:4]
                for hh in range(2):
                    ysc_ref[p, hh * rows + i:hh * rows + i + 1, :] = y8[hh:hh + 1]
        for p in range(n_pairs):
            y_ref[p, pl.ds(t0, rows), :] = jnp.concatenate(
                [ysc_ref[p, 0:rows, :], ysc_ref[p, rows:2 * rows, :]], axis=1)

    if tc <= SUBLANES:
        group(0)
    else:
        def body(g, carry):
            group(pl.multiple_of(g * SUBLANES, SUBLANES))
            return carry
        lax.fori_loop(0, tc // SUBLANES, body, 0)

    @pl.when(t_blk == pl.num_programs(2) - 1)
    def _():
        for p in range(n_pairs):
            sf_ref[0, p] = st_refs[p][...]


def _rwkv_scan(cfg, r, w, k, v, kk, b, s0, n_seq, t_len):
    c = cfg.rwkv_width
    n = cfg.rwkv_head_dim
    n_pairs_total = c // LANES
    pg = min(16, n_pairs_total)
    tc = _tile(t_len, 128)
    assert tc <= SUBLANES or tc % SUBLANES == 0
    bd2 = _head_pair_indicator(n)
    seq = pl.BlockSpec((pg, None, tc, LANES), lambda bi, g, t: (g, bi, t, 0))
    shp = lambda a: a.reshape(n_pairs_total, n_seq, t_len, LANES)
    st = pl.BlockSpec((1, pg, n, LANES), lambda bi, g, t: (bi, g, 0, 0))
    full = lambda bi, g, t: (0, 0)
    y, sf = pl.pallas_call(
        functools.partial(_rwkv_scan_kernel, n_pairs=pg, tc=tc),
        out_shape=(jax.ShapeDtypeStruct((n_pairs_total, n_seq, t_len, LANES), F32),
                   jax.ShapeDtypeStruct((n_seq, n_pairs_total, n, LANES), F32)),
        grid=(n_seq, n_pairs_total // pg, t_len // tc),
        in_specs=[seq] * 6 + [st, pl.BlockSpec((2 * LANES, LANES), full)],
        out_specs=(seq, st),
        scratch_shapes=[pltpu.VMEM((pg, 2 * SUBLANES, n), F32), pltpu.VMEM((LANES, LANES), F32)]
        + [pltpu.VMEM((n, LANES), F32)] * pg,
        compiler_params=_params("parallel", "parallel", "arbitrary"),
        name="rwkv_scan",
    )(shp(r), shp(w), shp(k), shp(v), shp(kk), shp(b), s0, bd2)
    return y.reshape(n_pairs_total, n_seq * t_len, LANES), sf


def _rwkv_post_kernel(y_ref, r_ref, k_ref, v_ref, g_ref, lnw_ref, lnb_ref, rk_ref, bd_ref, o_ref, *, n, eps):
    bd2 = bd_ref[...]
    for s in range(y_ref.shape[0]):
        sl = slice(s * LANES, (s + 1) * LANES)
        y = y_ref[s]
        mean = _seg_dot(y, bd2) / n
        d = y - mean
        var = _seg_dot(d * d, bd2) / n
        yn = d * lax.rsqrt(var + eps) * lnw_ref[:, sl] + lnb_ref[:, sl]
        bonus = _seg_dot(r_ref[s] * k_ref[s] * rk_ref[:, sl], bd2)
        yn = yn + bonus * v_ref[s]
        o_ref[:, sl] = (yn * g_ref[s]).astype(o_ref.dtype)


def _rwkv_post(cfg, y, r, k, v, g, prm):
    npair, m, _ = y.shape
    c = npair * LANES
    tm = _tile(m, 256)
    full = lambda i: (0, 0)
    big = pl.BlockSpec((npair, tm, LANES), lambda i: (0, i, 0))
    vec = pl.BlockSpec((1, c), full)
    return pl.pallas_call(
        functools.partial(_rwkv_post_kernel, n=float(cfg.rwkv_head_dim), eps=cfg.gn_eps),
        out_shape=jax.ShapeDtypeStruct((m, c), BF16),
        grid=(m // tm,),
        in_specs=[big] * 5 + [vec] * 3 + [pl.BlockSpec((2 * LANES, LANES), full)],
        out_specs=pl.BlockSpec((tm, c), lambda i: (i, 0)),
        compiler_params=_params("parallel"),
        name="rwkv_post",
    )(y, r, k, v, g, prm["ln_w"], prm["ln_b"], prm["rk"], _head_pair_indicator(cfg.rwkv_head_dim))


def _diff_lambda(lp, lambda_init):
    s1 = jnp.sum(lp[0:1] * lp[1:2], axis=-1, keepdims=True)
    s2 = jnp.sum(lp[2:3] * lp[3:4], axis=-1, keepdims=True)
    return jnp.exp(s1) - jnp.exp(s2) + lambda_init


def _softmax_rows(s):
    m = jnp.max(s, axis=-1, keepdims=True)
    e = jnp.exp(s - m)
    return e / jnp.sum(e, axis=-1, keepdims=True)


def _diff_prompt_kernel(lp_ref, q_ref, k_ref, v_ref, g_ref, o_ref, *, tq, d, lambda_init, eps):
    lam = _diff_lambda(lp_ref[...], lambda_init)
    s_len = q_ref.shape[0]
    kb = k_ref[...].astype(BF16)
    vb = v_ref[...].astype(BF16)
    scale = d ** -0.5
    nt = (((1,), (1,)), ((), ()))
    for qi in range(s_len // tq):
        nk = (qi + 1) * tq
        q = q_ref[qi * tq:(qi + 1) * tq, :].astype(BF16)
        qpos = qi * tq + lax.broadcasted_iota(jnp.int32, (tq, nk), 0)
        kpos = lax.broadcasted_iota(jnp.int32, (tq, nk), 1)
        ok = kpos <= qpos
        ps = []
        for c in range(2):
            s = lax.dot_general(q[:, c * d:(c + 1) * d], kb[:nk, c * d:(c + 1) * d], nt,
                                preferred_element_type=F32) * scale
            ps.append(_softmax_rows(jnp.where(ok, s, NEG_BIG)))
        attn = ps[0] - lam * ps[1]
        o = jnp.dot(attn.astype(BF16), vb[:nk], preferred_element_type=F32)
        o = o * lax.rsqrt(jnp.mean(o * o, axis=-1, keepdims=True) + eps) * g_ref[...]
        o_ref[qi * tq:(qi + 1) * tq, :] = (o * (1.0 - lambda_init)).astype(o_ref.dtype)


def _diff_prompt(cfg, lp, q, k, v, subln):
    m, wdt = q.shape
    hw = 2 * cfg.diff_head_dim
    s_len = cfg.seq
    blk = pl.BlockSpec((s_len, hw), lambda b, h: (b, h))
    return pl.pallas_call(
        functools.partial(_diff_prompt_kernel, tq=_tile(s_len, 256), d=cfg.diff_head_dim,
                          lambda_init=cfg.lambda_init, eps=cfg.subln_eps),
        out_shape=jax.ShapeDtypeStruct((m, wdt), BF16),
        grid=(m // s_len, wdt // hw),
        in_specs=[pl.BlockSpec(lp.shape, lambda b, h: (0, 0)), blk, blk, blk,
                  pl.BlockSpec((1, hw), lambda b, h: (0, 0))],
        out_specs=blk,
        compiler_params=_params("parallel", "parallel"),
        name="diff_prompt",
    )(lp, q, k, v, subln.reshape(1, hw))


def _diff_scores_kernel(pt_ref, q_ref, *refs, scale):
    del pt_ref
    k_refs, s_ref = refs[:-1], refs[-1]
    q = q_ref[0]
    ps = k_refs[0].shape[1]
    lane = lax.broadcasted_iota(jnp.int32, (q.shape[0], ps), 1)
    for r, k_ref in enumerate(k_refs):
        s = jnp.zeros((q.shape[0], ps), F32)
        for t in range(ps):
            col = jnp.sum(k_ref[0, t] * q, axis=-1, keepdims=True)
            s = jnp.where(lane == t, col, s)
        s_ref[0, :, r * ps:(r + 1) * ps] = s * scale


def _diff_weights_kernel(lp_ref, sa_ref, qa_ref, ka_ref, sb_ref, qb_ref, kb_ref, w_ref, wn_ref, *,
                         scale, lambda_init):
    lam = _diff_lambda(lp_ref[...], lambda_init)
    p, pn = [], []
    for s_r, q_r, k_r in ((sa_ref, qa_ref, ka_ref), (sb_ref, qb_ref, kb_ref)):
        s = s_r[0]
        sn = jnp.sum(q_r[0] * k_r[0], axis=-1, keepdims=True) * scale
        mx = jnp.maximum(jnp.max(s, axis=-1, keepdims=True), sn)
        e = jnp.exp(s - mx)
        en = jnp.exp(sn - mx)
        den = jnp.sum(e, axis=-1, keepdims=True) + en
        p.append(e / den)
        pn.append(en / den)
    w_ref[0] = p[0] - lam * p[1]
    wn_ref[0] = pn[0] - lam * pn[1]


def _diff_pv_kernel(pt_ref, w_ref, wn_ref, vn_ref, rep_ref, subln_ref, *refs, lambda_init, eps):
    del pt_ref
    v_refs, o_ref, acc_ref = refs[:-2], refs[-2], refs[-1]
    pg = pl.program_id(1)

    @pl.when(pg == 0)
    def _():
        acc_ref[...] = wn_ref[0].astype(BF16).astype(F32) * vn_ref[0].astype(BF16).astype(F32)

    _, ps, h, hw = v_refs[0].shape
    row = lax.broadcasted_iota(jnp.int32, (h, ps * h), 0)
    col = lax.broadcasted_iota(jnp.int32, (h, ps * h), 1)
    acc = acc_ref[...]
    for r, v_ref in enumerate(v_refs):
        w = w_ref[0, :, r * ps:(r + 1) * ps].astype(BF16)
        we = jnp.dot(w, rep_ref[...], preferred_element_type=F32)
        wd = jnp.where(col % h == row, we, 0.0).astype(BF16)
        acc = acc + jnp.dot(wd, v_ref[0].reshape(ps * h, hw).astype(BF16), preferred_element_type=F32)
    acc_ref[...] = acc

    @pl.when(pg == pl.num_programs(1) - 1)
    def _():
        o = acc_ref[...]
        ms = jnp.mean(o * o, axis=-1, keepdims=True)
        o_ref[0] = o * lax.rsqrt(ms + eps) * subln_ref[...] * (1.0 - lambda_init)


def _diff_sample(cfg, lp, q, k, v, cache_k, cache_v, page_table, subln):
    db, wdt = q.shape
    h = cfg.diff_heads
    d = cfg.diff_head_dim
    hw = 2 * d
    ps = cfg.page_size
    n_pages = page_table.shape[1]
    n_past = n_pages * ps
    scale = d ** -0.5
    q3, k3 = q.reshape(db, 2 * h, d), k.reshape(db, 2 * h, d)
    v3 = v.reshape(db, h, hw)
    pps = _tile(n_pages, 4)
    rows = lambda nr, nc: pl.BlockSpec((1, nr, nc), lambda b, p, pt: (b, 0, 0))

    def pages(nr, nc):
        return [pl.BlockSpec((1, ps, nr, nc), lambda b, p, pt, r=r: (pt[b, p * pps + r], 0, 0, 0))
                for r in range(pps)]

    s = pl.pallas_call(
        functools.partial(_diff_scores_kernel, scale=scale),
        out_shape=jax.ShapeDtypeStruct((db, 2 * h, n_past), F32),
        grid_spec=pltpu.PrefetchScalarGridSpec(
            num_scalar_prefetch=1, grid=(db, n_pages // pps),
            in_specs=[rows(2 * h, d)] + pages(2 * h, d),
            out_specs=pl.BlockSpec((1, 2 * h, pps * ps), lambda b, p, pt: (b, 0, p))),
        compiler_params=_params("parallel", "arbitrary"),
        name="diff_scores",
    )(page_table, q3, *([cache_k] * pps))
    maps = lambda a: [a.reshape(db, h, 2, a.shape[-1])[:, :, c] for c in range(2)]
    (qa, qb), (ka, kb), (sa, sb) = maps(q3), maps(k3), maps(s)
    headb = pl.BlockSpec((1, h, d), lambda b: (b, 0, 0))
    pastb = pl.BlockSpec((1, h, n_past), lambda b: (b, 0, 0))
    w, wn = pl.pallas_call(
        functools.partial(_diff_weights_kernel, scale=scale, lambda_init=cfg.lambda_init),
        out_shape=(jax.ShapeDtypeStruct((db, h, n_past), F32), jax.ShapeDtypeStruct((db, h, 1), F32)),
        grid=(db,),
        in_specs=[pl.BlockSpec(lp.shape, lambda b: (0, 0)), pastb, headb, headb, pastb, headb, headb],
        out_specs=(pastb, pl.BlockSpec((1, h, 1), lambda b: (b, 0, 0))),
        compiler_params=_params("parallel"),
        name="diff_weights",
    )(lp, sa, qa, ka, sb, qb, kb)
    repeat = (jnp.arange(ps)[:, None] == (jnp.arange(ps * h) // h)[None, :]).astype(BF16)
    o = pl.pallas_call(
        functools.partial(_diff_pv_kernel, lambda_init=cfg.lambda_init, eps=cfg.subln_eps),
        out_shape=jax.ShapeDtypeStruct((db, h, hw), F32),
        grid_spec=pltpu.PrefetchScalarGridSpec(
            num_scalar_prefetch=1, grid=(db, n_pages // pps),
            in_specs=[pl.BlockSpec((1, h, pps * ps), lambda b, p, pt: (b, 0, p)),
                      rows(h, 1), rows(h, hw),
                      pl.BlockSpec((ps, ps * h), lambda b, p, pt: (0, 0)),
                      pl.BlockSpec((1, hw), lambda b, p, pt: (0, 0))] + pages(h, hw),
            out_specs=rows(h, hw),
            scratch_shapes=[pltpu.VMEM((h, hw), F32)]),
        compiler_params=_params("parallel", "arbitrary"),
        name="diff_pv",
    )(page_table, w, wn, v3, repeat, subln.reshape(1, hw), *([cache_v] * pps))
    return o.reshape(db, wdt)


def _top_mask_lanes(gate, n_top):
    nb = gate.shape[1]
    lane = lax.broadcasted_iota(jnp.int32, gate.shape, 1).astype(F32)
    sel = jnp.zeros(gate.shape, F32)
    for _ in range(n_top):
        mx = jnp.max(gate, axis=1, keepdims=True)
        idx = jnp.min(jnp.where(gate == mx, lane, float(nb)), axis=1, keepdims=True)
        hit = (lane == idx) & (mx > 0.5 * NEG_BIG)
        sel = jnp.where(hit, 1.0, sel)
        gate = jnp.where(lane == idx, NEG_BIG, gate)
    return sel


def _moba_prompt_kernel(q_ref, k_ref, v_ref, ext_ref, o_ref, *, bs, n_top):
    s_len, d = q_ref.shape
    nb = s_len // bs
    kf = k_ref[...]
    vb = v_ref[...].astype(BF16)
    kmean = jnp.sum(kf.reshape(nb, bs, d), axis=1) / bs
    kmean = jnp.concatenate([kmean, jnp.zeros((LANES - nb, d), F32)], axis=0)
    k_aug = jnp.concatenate([kf.astype(BF16), ext_ref[...]], axis=1)
    scale = d ** -0.5
    nt = (((1,), (1,)), ((), ()))
    causal = (lax.broadcasted_iota(jnp.int32, (bs, bs), 1) <= lax.broadcasted_iota(jnp.int32, (bs, bs), 0))
    blk = lax.broadcasted_iota(jnp.int32, (bs, LANES), 1)
    gate_all = lax.dot_general(q_ref[...], kmean, nt, precision=lax.Precision.HIGHEST,
                               preferred_element_type=F32)
    for qi in range(nb):
        lo, nk = qi * bs, (qi + 1) * bs
        qf = q_ref[lo:nk, :]
        allowed = blk == qi
        if qi > 0:
            gate = gate_all[lo:nk]
            allowed = allowed | (_top_mask_lanes(jnp.where(blk < qi, gate, NEG_BIG), n_top) > 0.5)
        bias = jnp.where(allowed, 0.0, NEG_BIG).astype(BF16)
        q_aug = jnp.concatenate([(qf * scale).astype(BF16), bias], axis=1)
        s = lax.dot_general(q_aug, k_aug[:nk], nt, preferred_element_type=F32)
        s_own = jnp.where(causal, s[:, lo:nk], NEG_BIG)
        mx = jnp.max(s_own, axis=-1, keepdims=True)
        if qi > 0:
            mx = jnp.maximum(mx, jnp.max(s[:, :lo], axis=-1, keepdims=True))
        e_own = jnp.exp(s_own - mx)
        den = jnp.sum(e_own, axis=-1, keepdims=True)
        o = jnp.dot(e_own.astype(BF16), vb[lo:nk], preferred_element_type=F32)
        if qi > 0:
            e_past = jnp.exp(s[:, :lo] - mx)
            den = den + jnp.sum(e_past, axis=-1, keepdims=True)
            o = o + jnp.dot(e_past.astype(BF16), vb[:lo], preferred_element_type=F32)
        o_ref[lo:nk, :] = (o / den).astype(o_ref.dtype)


def _moba_prompt(cfg, q, k, v):
    m, wdt = q.shape
    d = cfg.moba_head_dim
    s_len = cfg.seq
    bs = cfg.moba_block
    nb = s_len // bs
    assert nb <= LANES
    onehot = ((jnp.arange(s_len) // bs)[:, None] == jnp.arange(LANES)[None, :]).astype(BF16)
    blk = pl.BlockSpec((s_len, d), lambda b, h: (b, h))
    return pl.pallas_call(
        functools.partial(_moba_prompt_kernel, bs=bs, n_top=min(cfg.moba_topk, nb)),
        out_shape=jax.ShapeDtypeStruct((m, wdt), BF16),
        grid=(m // s_len, wdt // d),
        in_specs=[blk, blk, blk, pl.BlockSpec((s_len, LANES), lambda b, h: (0, 0))],
        out_specs=blk,
        compiler_params=_params("parallel", "parallel"),
        name="moba_prompt",
    )(q, k, v, onehot)


def _moba_select_kernel(pt_ref, q_ref, kn_ref, *refs, ppb, bs, n_past_blocks, own_block, n_top):
    del pt_ref
    k_refs, sel_ref, gate_ref = refs[:ppb], refs[ppb], refs[ppb + 1]
    blk = pl.program_id(1)
    q = q_ref[0]
    lane = lax.broadcasted_iota(jnp.int32, gate_ref.shape, 1)

    @pl.when(blk == 0)
    def _():
        gate_ref[...] = jnp.zeros(gate_ref.shape, F32)

    def body(t, tot):
        for r in k_refs:
            tot = tot + r[0, t]
        return tot

    tot = lax.fori_loop(0, k_refs[0].shape[1], body, jnp.zeros(q.shape, F32), unroll=8)
    g = jnp.sum(tot * q, axis=-1, keepdims=True) / bs
    gate_ref[...] = jnp.where(lane == blk, g, gate_ref[...])

    @pl.when(blk == pl.num_programs(1) - 1)
    def _():
        g_own = jnp.sum(kn_ref[0] * q, axis=-1, keepdims=True) / bs
        gate = gate_ref[...] + jnp.where(lane == own_block, g_own, 0.0)
        gate = jnp.where((lane < own_block) & (lane < n_past_blocks), gate, NEG_BIG)
        lanef = lane.astype(F32)
        out = jnp.zeros(gate.shape, F32)
        for j in range(n_top):
            mx = jnp.max(gate, axis=1, keepdims=True)
            idx = jnp.min(jnp.where(gate == mx, lanef, float(LANES)), axis=1, keepdims=True)
            out = jnp.where(lane == j, idx, out)
            gate = jnp.where(lanef == idx, NEG_BIG, gate)
        sel_ref[0] = out.astype(jnp.int32)


def _moba_sample_attn_kernel(sel_ref, pt_ref, q_ref, kn_ref, vn_ref, *refs, n_pg, scale, hg):
    del sel_ref, pt_ref
    k_refs, v_refs, o_ref = refs[:n_pg], refs[n_pg:2 * n_pg], refs[2 * n_pg]
    hh = pl.program_id(1) % hg
    ps = k_refs[0].shape[1]
    q = q_ref[0, 0]
    row = lax.broadcasted_iota(jnp.int32, (hg, ps), 0)
    lane = lax.broadcasted_iota(jnp.int32, (hg, ps), 1)

    def page_scores(r):
        s8 = jnp.zeros((hg, ps), F32)
        for t in range(ps):
            col = jnp.sum(r[0, t] * q, axis=-1, keepdims=True)
            s8 = jnp.where(lane == t, col, s8)
        return jnp.sum(jnp.where(row == hh, s8, 0.0), axis=0, keepdims=True) * scale

    s = [page_scores(r) for r in k_refs]
    sn = jnp.sum(q * kn_ref[0, 0], axis=-1, keepdims=True) * scale
    mx = sn
    for x in s:
        mx = jnp.maximum(mx, jnp.max(x, axis=-1, keepdims=True))
    e = [jnp.exp(x - mx) for x in s]
    en = jnp.exp(sn - mx)
    den = en
    for x in e:
        den = den + jnp.sum(x, axis=-1, keepdims=True)
    n_acc = 4
    accs = [jnp.zeros((hg, q.shape[1]), F32)] * n_acc
    for x, r in zip(e, v_refs):
        for t in range(ps):
            accs[t % n_acc] = accs[t % n_acc] + r[0, t] * x[:, t:t + 1]
    acc = (accs[0] + accs[1]) + (accs[2] + accs[3])
    row_d = lax.broadcasted_iota(jnp.int32, acc.shape, 0)
    o = jnp.sum(jnp.where(row_d == hh, acc, 0.0), axis=0, keepdims=True) + en * vn_ref[0, 0]
    o_ref[0, 0] = o / den


def _moba_sample(cfg, q, k, v, cache_k, cache_v, page_table):
    db, wdt = q.shape
    d = cfg.moba_head_dim
    nh = cfg.moba_heads
    ps, bs = cfg.page_size, cfg.moba_block
    ppb = bs // ps
    n_pages = page_table.shape[1]
    n_pool = cache_k.shape[0]
    assert n_pages % ppb == 0 and cfg.past_len == n_pages * ps
    nbp = n_pages // ppb
    own = cfg.past_len // bs
    n_top = cfg.moba_topk
    assert nbp >= n_top and own == nbp and own < LANES
    hg = min(SUBLANES, nh)
    assert nh % hg == 0
    q3, k3 = q.reshape(db, nh, d), k.reshape(db, nh, d)

    def page_spec(r):
        return pl.BlockSpec((1, ps, nh, d), lambda b, n, pt: (pt[b, n * ppb + r], 0, 0, 0))

    allh = pl.BlockSpec((1, nh, d), lambda b, n, pt: (b, 0, 0))
    sel = pl.pallas_call(
        functools.partial(_moba_select_kernel, ppb=ppb, bs=float(bs), n_past_blocks=nbp, own_block=own,
                          n_top=n_top),
        out_shape=jax.ShapeDtypeStruct((db, nh, LANES), jnp.int32),
        grid_spec=pltpu.PrefetchScalarGridSpec(
            num_scalar_prefetch=1, grid=(db, nbp),
            in_specs=[allh, allh] + [page_spec(r) for r in range(ppb)],
            out_specs=pl.BlockSpec((1, nh, LANES), lambda b, n, pt: (b, 0, 0)),
            scratch_shapes=[pltpu.VMEM((nh, LANES), F32)]),
        compiler_params=_params("parallel", "arbitrary"),
        name="moba_select",
    )(page_table, q3, k3, *([cache_k] * ppb))
    sel_flat = sel[:, :, :n_top].reshape(-1)
    pt_flat = page_table.reshape(-1)
    n_pg = n_top * ppb

    def sel_page(jp):
        j, r = divmod(jp, ppb)
        return pl.BlockSpec(
            (1, ps, hg, d),
            lambda b, h, sl, pt: (pt[b * n_pages + sl[(b * nh + h) * n_top + j] * ppb + r], 0, h // hg, 0))

    q4, k4, v4 = (a.reshape(db, nh, 1, d) for a in (q, k, v))
    headb = pl.BlockSpec((1, 1, 1, d), lambda b, h, sl, pt: (b, h, 0, 0))
    o = pl.pallas_call(
        functools.partial(_moba_sample_attn_kernel, n_pg=n_pg, scale=d ** -0.5, hg=hg),
        out_shape=jax.ShapeDtypeStruct((db, nh, 1, d), F32),
        grid_spec=pltpu.PrefetchScalarGridSpec(
            num_scalar_prefetch=2, grid=(db, nh),
            in_specs=[headb, headb, headb] + [sel_page(jp) for jp in range(n_pg)] * 2,
            out_specs=headb),
        compiler_params=_params("parallel", "parallel"),
        name="moba_sample_attn",
    )(sel_flat, pt_flat, q4, k4, v4, *([cache_k] * n_pg), *([cache_v] * n_pg))
    return o.reshape(db, wdt)


def _rope_tables(cfg, pos, d):
    half = d // 2
    inv = cfg.rope_theta ** (-jnp.arange(half, dtype=F32) / half)
    ang = pos.astype(F32)[:, None] * inv[None, :]
    cos, sin = jnp.cos(ang), jnp.sin(ang)
    return jnp.concatenate([cos, cos], axis=-1), jnp.concatenate([-sin, sin], axis=-1)


def _rwkv_layout(cfg):
    c = cfg.rwkv_width
    segs, o = [], 0
    for wd in (c, c, c, cfg.lora_w, cfg.lora_a, cfg.lora_g):
        segs.append((o, wd, _rup(wd, LANES)))
        o += wd
    return segs, o


def _pad_cols(cfg, a):
    segs, _ = _rwkv_layout(cfg)
    parts = []
    for o, wd, pw in segs:
        x = a[..., o:o + wd]
        if pw != wd:
            x = jnp.pad(x, [(0, 0)] * (a.ndim - 1) + [(0, pw - wd)])
        parts.append(x)
    return jnp.concatenate(parts, axis=-1)


def _unpad_cols(cfg, a):
    segs, _ = _rwkv_layout(cfg)
    parts, o = [], 0
    for _, wd, pw in segs:
        parts.append(a[..., o:o + wd])
        o += pw
    return jnp.concatenate(parts, axis=-1)


def _pad_rows(a, rows):
    return jnp.pad(a, ((0, rows - a.shape[0]), (0, 0)))


def _state_to_pairs(cfg, s):
    b, h, n, _ = s.shape
    return jnp.transpose(s.reshape(b, h // 2, 2, n, n), (0, 1, 3, 2, 4)).reshape(b, h // 2, n, 2 * n)


def _pairs_to_state(cfg, s):
    b, hp, n, _ = s.shape
    return jnp.transpose(s.reshape(b, hp, n, 2, n), (0, 1, 3, 2, 4)).reshape(b, 2 * hp, n, n)


def _forward(cfg, x_prompt, x_sample, cache_k0, cache_v0, cache_k1, cache_v1, page_table,
             state_wkv0, state_shift0, state_conv,
             norm_mix, norm_ffn, norm_final, w_in0, w_out0,
             rwkv_mu, rwkv_w0, rwkv_w2, rwkv_a0, rwkv_a2, rwkv_g2, rwkv_kk, rwkv_ka, rwkv_rk,
             rwkv_ln_w, rwkv_ln_b, diff_lq1, diff_lk1, diff_lq2, diff_lk2, diff_subln,
             w_in1, w_out1, ffn_w_up, ffn_conv_w, ffn_conv_b, ffn_w_down):
    bsz, s_len, dm = x_prompt.shape
    db = x_sample.shape[0]
    c = cfg.rwkv_width
    dw = dm - c
    segs, rproj = _rwkv_layout(cfg)
    n_heads = c // cfg.rwkv_head_dim

    w_r = _pad_cols(cfg, w_in0[:, :rproj]).astype(BF16)
    w_q, w_k, w_v = (w_in0[:, rproj + i * dw:rproj + (i + 1) * dw].astype(BF16) for i in range(3))
    w_out0_b, w_in1_b, w_out1_b, w_up_b = w_out0, w_in1, w_out1, ffn_w_up
    w_down_b = ffn_w_down.astype(BF16)
    lw_p, la_p = _rup(cfg.lora_w, LANES), _rup(cfg.lora_a, LANES)
    row = lambda a: a.reshape(1, -1)
    prm = dict(mu=row(_pad_cols(cfg, rwkv_mu)), w0=row(rwkv_w0), a0=row(rwkv_a0), kk=row(rwkv_kk),
               ka=row(rwkv_ka), w2=_pad_rows(rwkv_w2, lw_p).astype(BF16),
               a2=_pad_rows(rwkv_a2, la_p).astype(BF16),
               g2=_pad_rows(rwkv_g2, _rup(cfg.lora_g, LANES)).astype(BF16),
               ln_w=row(rwkv_ln_w), ln_b=row(rwkv_ln_b), rk=row(rwkv_rk))
    lp = jnp.stack([diff_lq1, diff_lk1, diff_lq2, diff_lk2])
    rope_p = _rope_tables(cfg, jnp.arange(s_len, dtype=jnp.int32), cfg.diff_head_dim)
    rope_s = _rope_tables(cfg, jnp.full((db,), cfg.past_len, jnp.int32), cfg.diff_head_dim)

    xp = x_prompt.reshape(bsz * s_len, dm)
    xs = x_sample.reshape(db, dm)
    out = {}
    conv_p, conv_s = [], []
    for layer in range(2):
        hp = _rmsnorm(xp, norm_mix[layer], cfg.norm_eps, BF16)
        hs = _rmsnorm(xs, norm_mix[layer], cfg.norm_eps, BF16)
        if layer == 0:
            mix = {}
            for tag, h, n_seq, t_len, rope in (("p", hp, bsz, s_len, rope_p), ("s", hs, db, 1, rope_s)):
                p = _matmul(h, w_r)
                if tag == "p":
                    p_prev = None
                    s0 = jnp.zeros((n_seq, n_heads // 2, cfg.rwkv_head_dim, LANES), F32)
                    shift = p.reshape(n_seq, t_len, -1)[:, -1]
                else:
                    p_prev = _pad_cols(cfg, state_shift0)
                    s0 = _state_to_pairs(cfg, state_wkv0)
                    shift = p
                r, w, k, v, kk, b, g = _rwkv_prep(cfg, p, p_prev, prm, rows_per_seq=t_len)
                y, sf = _rwkv_scan(cfg, r, w, k, v, kk, b, s0, n_seq, t_len)
                a_out = _rwkv_post(cfg, y, r, k, v, g, prm)
                q0 = _matmul(h, w_q, rope=rope, rows_per_seq=t_len if tag == "p" else db)
                k0 = _matmul(h, w_k, rope=rope, rows_per_seq=t_len if tag == "p" else db)
                v0 = _matmul(h, w_v)
                if tag == "p":
                    b_out = _diff_prompt(cfg, lp, q0, k0, v0, diff_subln)
                else:
                    b_out = _diff_sample(cfg, lp, q0, k0, v0, cache_k0, cache_v0, page_table,
                                         diff_subln).astype(BF16)
                mix[tag] = (a_out, b_out)
                out["k0" + tag] = k0.reshape(n_seq, t_len, 2 * cfg.diff_heads, cfg.diff_head_dim)
                out["v0" + tag] = v0.reshape(n_seq, t_len, cfg.diff_heads, 2 * cfg.diff_head_dim)
                out["wkv" + tag] = _pairs_to_state(cfg, sf)
                out["shift" + tag] = _unpad_cols(cfg, shift)
            xp = _matmul(mix["p"][0], w_out0_b, x2=mix["p"][1], w2=w_out0_b, w2_row0=c, res=xp)
            xs = _matmul(mix["s"][0], w_out0_b, x2=mix["s"][1], w2=w_out0_b, w2_row0=c, res=xs)
        else:
            qkv = []
            for tag, h, n_seq, t_len, rope in (("p", hp, bsz, s_len, rope_p), ("s", hs, db, 1, rope_s)):
                rps = t_len if tag == "p" else db
                q1 = _matmul(h, w_in1_b, cols=(0, dm), rope=rope, rows_per_seq=rps)
                k1 = _matmul(h, w_in1_b, cols=(dm, dm), rope=rope, rows_per_seq=rps)
                v1 = _matmul(h, w_in1_b, cols=(2 * dm, dm))
                out["k1" + tag] = k1.reshape(n_seq, t_len, cfg.moba_heads, cfg.moba_head_dim)
                out["v1" + tag] = v1.reshape(n_seq, t_len, cfg.moba_heads, cfg.moba_head_dim)
                qkv.append((q1, k1, v1))
            mp = _moba_prompt(cfg, *qkv[0])
            ms = _moba_sample(cfg, *qkv[1], cache_k1, cache_v1, page_table).astype(BF16)
            xp = _matmul(mp, w_out1_b, res=xp)
            xs = _matmul(ms, w_out1_b, res=xs)
        hp = _rmsnorm(xp, norm_ffn[layer], cfg.norm_eps, BF16)
        hs = _rmsnorm(xs, norm_ffn[layer], cfg.norm_eps, BF16)
        fp, cp = _ffn_up(hp, w_up_b, layer, ffn_conv_w[layer], ffn_conv_b[layer], s_len)
        prev = state_conv[layer]
        fs, gs = _ffn_up_step(hs, w_up_b, layer, ffn_conv_w[layer], ffn_conv_b[layer], prev[:, 0], prev[:, 1])
        xp = _matmul(fp, w_down_b, w_lead=layer, res=xp, tm=512, tn=256)
        xs = _matmul(fs, w_down_b, w_lead=layer, res=xs, tm=512, tn=256)
        conv_p.append(cp)
        conv_s.append(jnp.stack([prev[:, 1], gs], axis=1))
    y_prompt = _rmsnorm(xp, norm_final, cfg.norm_eps, F32).reshape(bsz, s_len, dm)
    y_sample = _rmsnorm(xs, norm_final, cfg.norm_eps, F32).reshape(db, 1, dm)
    return (y_prompt, y_sample, out["k0p"], out["v0p"], out["k0s"], out["v0s"],
            out["wkvp"], out["wkvs"], out["shiftp"], out["shifts"],
            out["k1p"], out["v1p"], out["k1s"], out["v1s"], jnp.stack(conv_p), jnp.stack(conv_s))


def kernel(x_prompt, x_sample, cache_k0, cache_v0, cache_k1, cache_v1, page_table, state_wkv0, state_shift0, state_conv, norm_mix, norm_ffn, norm_final, w_in0, w_out0, rwkv_mu, rwkv_w0, rwkv_w2, rwkv_a0, rwkv_a2, rwkv_g2, rwkv_kk, rwkv_ka, rwkv_rk, rwkv_ln_w, rwkv_ln_b, diff_lq1, diff_lk1, diff_lq2, diff_lk2, diff_subln, w_in1, w_out1, ffn_w_up, ffn_conv_w, ffn_conv_b, ffn_w_down):
    return _forward(_CFG, x_prompt, x_sample, cache_k0, cache_v0, cache_k1, cache_v1, page_table,
                    state_wkv0, state_shift0, state_conv, norm_mix, norm_ffn, norm_final, w_in0, w_out0,
                    rwkv_mu, rwkv_w0, rwkv_w2, rwkv_a0, rwkv_a2, rwkv_g2, rwkv_kk, rwkv_ka, rwkv_rk,
                    rwkv_ln_w, rwkv_ln_b, diff_lq1, diff_lk1, diff_lq2, diff_lk2, diff_subln,
                    w_in1, w_out1, ffn_w_up, ffn_conv_w, ffn_conv_b, ffn_w_down)
```

```python
import functools
import math
from typing import NamedTuple

import jax
import jax.numpy as jnp
from jax import lax
from jax.experimental import pallas as pl
from jax.experimental.pallas import tpu as pltpu

F32 = jnp.float32
BF16 = jnp.bfloat16
LANES = 128
SUBLANES = 8
NEG_BIG = -1e30
VMEM_LIMIT = 56 * 1024 * 1024


class Cfg(NamedTuple):
    d_model: int
    batch: int
    seq: int
    dec_batch: int
    past_len: int
    page_size: int
    rwkv_width: int
    rwkv_head_dim: int
    lora_w: int
    lora_a: int
    lora_g: int
    diff_heads: int
    diff_head_dim: int
    moba_heads: int
    moba_head_dim: int
    moba_block: int
    moba_topk: int
    d_ff: int
    rope_theta: float
    norm_eps: float
    subln_eps: float
    gn_eps: float
    lambda_init: float


def _make_cfg(d_model=4096, batch=4, seq=2048, dec_batch=8, past_len=16384, page_size=128, d_ff=11008,
              diff_heads=8, moba_block=256):
    rw = d_model // 2
    lw = max(32, int(round(1.8 * rw ** 0.5 / 32)) * 32)
    lg = max(32, int(round(0.6 * rw ** 0.8 / 32)) * 32)
    return Cfg(d_model=d_model, batch=batch, seq=seq, dec_batch=dec_batch, past_len=past_len,
               page_size=page_size, rwkv_width=rw, rwkv_head_dim=64, lora_w=lw, lora_a=lw, lora_g=lg,
               diff_heads=diff_heads, diff_head_dim=(d_model - rw) // (2 * diff_heads),
               moba_heads=d_model // 128, moba_head_dim=128, moba_block=moba_block, moba_topk=3,
               d_ff=d_ff, rope_theta=10000.0, norm_eps=1e-6, subln_eps=1e-5, gn_eps=64e-5,
               lambda_init=0.8 - 0.6 * math.exp(-0.3 * 0))


_CFG = _make_cfg()


def _rup(n, m):
    return -(-n // m) * m


def _tile(n, pref):
    if n <= pref:
        return n
    t = pref
    while n % t:
        t -= SUBLANES
    return t


def _params(*sem):
    return pltpu.CompilerParams(dimension_semantics=sem, vmem_limit_bytes=VMEM_LIMIT)


def _split_bf16(x):
    hi = x.astype(BF16)
    lo = (x - hi.astype(F32)).astype(BF16)
    return hi, lo


def _seg_dot(x, ind2):
    hi, lo = _split_bf16(x)
    return jnp.dot(jnp.concatenate([hi, lo], axis=1), ind2, preferred_element_type=F32)


def _rms_kernel(x_ref, g_ref, o_ref, *, eps):
    x = x_ref[...]
    ms = jnp.mean(x * x, axis=-1, keepdims=True)
    o_ref[...] = (x * lax.rsqrt(ms + eps) * g_ref[...]).astype(o_ref.dtype)


def _rmsnorm(x, g, eps, out_dtype):
    m, d = x.shape
    tm = _tile(m, 256)
    return pl.pallas_call(
        functools.partial(_rms_kernel, eps=eps),
        out_shape=jax.ShapeDtypeStruct((m, d), out_dtype),
        grid=(m // tm,),
        in_specs=[pl.BlockSpec((tm, d), lambda i: (i, 0)), pl.BlockSpec((1, d), lambda i: (0, 0))],
        out_specs=pl.BlockSpec((tm, d), lambda i: (i, 0)),
        compiler_params=_params("parallel"),
        name="rmsnorm",
    )(x, g.reshape(1, d))


def _mm_kernel(*refs, has_res, has_rope, two_lhs):
    it = iter(refs)
    x_ref = next(it)
    x2_ref = next(it) if two_lhs else None
    w_ref = next(it)
    w2_ref = next(it) if two_lhs else None
    res_ref = next(it) if has_res else None
    cos_ref = next(it) if has_rope else None
    sin_ref = next(it) if has_rope else None
    o_ref = next(it)
    acc = jnp.dot(x_ref[...], w_ref[...].astype(BF16), preferred_element_type=F32)
    if two_lhs:
        acc = acc + jnp.dot(x2_ref[...], w2_ref[...].astype(BF16), preferred_element_type=F32)
    if has_rope:
        cos = cos_ref[...]
        sin = sin_ref[...]
        for s in range(acc.shape[1] // LANES):
            a = acc[:, s * LANES:(s + 1) * LANES]
            o_ref[:, s * LANES:(s + 1) * LANES] = (
                a * cos + pltpu.roll(a, LANES // 2, axis=1) * sin).astype(o_ref.dtype)
    else:
        if has_res:
            acc = acc + res_ref[...]
        o_ref[...] = acc.astype(o_ref.dtype)


def _matmul(x, w, *, w_lead=None, cols=None, x2=None, w2=None, w2_row0=0, res=None, rope=None,
            rows_per_seq=None, out_dtype=F32, tm=1024, tn=512):
    m, k = x.shape
    col0, n = cols if cols is not None else (0, w.shape[-1])
    tm = _tile(m if rope is None else rows_per_seq, tm)
    tn = _tile(n, tn)
    assert col0 % tn == 0
    jb = col0 // tn
    two = x2 is not None
    in_specs = [pl.BlockSpec((tm, k), lambda i, j: (i, 0))]
    args = [x]
    if two:
        in_specs.append(pl.BlockSpec((tm, x2.shape[1]), lambda i, j: (i, 0)))
        args.append(x2)
    if w.ndim == 3:
        in_specs.append(pl.BlockSpec((None, k, tn), lambda i, j: (w_lead, 0, jb + j)))
    else:
        in_specs.append(pl.BlockSpec((k, tn), lambda i, j: (0, jb + j)))
    args.append(w)
    if two:
        k2 = x2.shape[1]
        assert w2_row0 % k2 == 0
        in_specs.append(pl.BlockSpec((k2, tn), lambda i, j: (w2_row0 // k2, jb + j)))
        args.append(w2)
    if res is not None:
        in_specs.append(pl.BlockSpec((tm, tn), lambda i, j: (i, j)))
        args.append(res)
    if rope is not None:
        tps = rows_per_seq // tm
        for t in rope:
            in_specs.append(pl.BlockSpec((tm, LANES), lambda i, j: (i % tps, 0)))
            args.append(t)
    return pl.pallas_call(
        functools.partial(_mm_kernel, has_res=res is not None, has_rope=rope is not None, two_lhs=two),
        out_shape=jax.ShapeDtypeStruct((m, n), out_dtype),
        grid=(m // tm, n // tn),
        in_specs=in_specs,
        out_specs=pl.BlockSpec((tm, tn), lambda i, j: (i, j)),
        compiler_params=_params("parallel", "parallel"),
        name="matmul",
    )(*args)


def _ffn_up_kernel(x_ref, wg_ref, wu_ref, cw_ref, cb_ref, h_ref, cs_ref, carry_ref, *, tiles_per_seq):
    i = pl.program_id(0)
    j = pl.program_id(1)
    x = x_ref[...]
    g = jnp.dot(x, wg_ref[...].astype(BF16), preferred_element_type=F32)
    u = jnp.dot(x, wu_ref[...].astype(BF16), preferred_element_type=F32)
    tm = g.shape[0]

    @pl.when(i % tiles_per_seq == 0)
    def _():
        carry_ref[j] = jnp.zeros(carry_ref.shape[1:], F32)

    carry = carry_ref[j]
    row = lax.broadcasted_iota(jnp.int32, g.shape, 0)
    g1 = jnp.where(row == 0, carry[1:2], pltpu.roll(g, 1, axis=0))
    g2 = jnp.where(row == 0, carry[0:1], jnp.where(row == 1, carry[1:2], pltpu.roll(g, 2, axis=0)))
    cw = cw_ref[...]
    gc = cb_ref[...] + cw[0:1] * g2
    gc = gc + cw[1:2] * g1
    gc = gc + cw[2:3] * g
    h_ref[...] = (gc * jax.nn.sigmoid(gc) * u).astype(h_ref.dtype)
    last2 = g[tm - 2:tm, :]
    carry_ref[j, 0:2, :] = last2
    cs_ref[0] = last2


def _ffn_up(x, w_up, layer, conv_w, conv_b, rows_per_seq, tm=1024, tn=256):
    m, d = x.shape
    f = w_up.shape[2] // 2
    tm = _tile(rows_per_seq, tm)
    tn = _tile(f, tn)
    nj = f // tn
    tps = rows_per_seq // tm
    h, tails = pl.pallas_call(
        functools.partial(_ffn_up_kernel, tiles_per_seq=tps),
        out_shape=(jax.ShapeDtypeStruct((m, f), BF16),
                   jax.ShapeDtypeStruct((m // tm, 2, f), F32)),
        grid=(m // tm, nj),
        in_specs=[pl.BlockSpec((tm, d), lambda i, j: (i, 0)),
                  pl.BlockSpec((None, d, tn), lambda i, j: (layer, 0, j)),
                  pl.BlockSpec((None, d, tn), lambda i, j: (layer, 0, nj + j)),
                  pl.BlockSpec((3, tn), lambda i, j: (0, j)),
                  pl.BlockSpec((1, tn), lambda i, j: (0, j))],
        out_specs=(pl.BlockSpec((tm, tn), lambda i, j: (i, j)),
                   pl.BlockSpec((1, 2, tn), lambda i, j: (i, 0, j))),
        scratch_shapes=[pltpu.VMEM((nj, SUBLANES, tn), F32)],
        compiler_params=_params("arbitrary", "arbitrary"),
        name="ffn_up",
    )(x, w_up, w_up, conv_w, conv_b.reshape(1, f))
    return h, tails[tps - 1::tps]


def _ffn_up_step_kernel(x_ref, wg_ref, wu_ref, cw_ref, cb_ref, p0_ref, p1_ref, h_ref, g_ref):
    x = x_ref[...]
    g = jnp.dot(x, wg_ref[...].astype(BF16), preferred_element_type=F32)
    u = jnp.dot(x, wu_ref[...].astype(BF16), preferred_element_type=F32)
    cw = cw_ref[...]
    gc = cb_ref[...] + cw[0:1] * p0_ref[...]
    gc = gc + cw[1:2] * p1_ref[...]
    gc = gc + cw[2:3] * g
    h_ref[...] = (gc * jax.nn.sigmoid(gc) * u).astype(h_ref.dtype)
    g_ref[...] = g


def _ffn_up_step(x, w_up, layer, conv_w, conv_b, prev0, prev1, tn=256):
    m, d = x.shape
    f = w_up.shape[2] // 2
    tn = _tile(f, tn)
    nj = f // tn
    row = lambda j: (0, j)
    return pl.pallas_call(
        _ffn_up_step_kernel,
        out_shape=(jax.ShapeDtypeStruct((m, f), BF16), jax.ShapeDtypeStruct((m, f), F32)),
        grid=(nj,),
        in_specs=[pl.BlockSpec((m, d), lambda j: (0, 0)),
                  pl.BlockSpec((None, d, tn), lambda j: (layer, 0, j)),
                  pl.BlockSpec((None, d, tn), lambda j: (layer, 0, nj + j)),
                  pl.BlockSpec((3, tn), row),
                  pl.BlockSpec((1, tn), row),
                  pl.BlockSpec((m, tn), row),
                  pl.BlockSpec((m, tn), row)],
        out_specs=(pl.BlockSpec((m, tn), row), pl.BlockSpec((m, tn), row)),
        compiler_params=_params("parallel"),
        name="ffn_up_step",
    )(x, w_up, w_up, conv_w, conv_b.reshape(1, f), prev0, prev1)


def _softplus(x):
    return jnp.maximum(x, 0.0) + jnp.log(1.0 + jnp.exp(-jnp.abs(x)))


def _rwkv_prep_kernel(p_ref, pp_ref, mu_ref, w0_ref, a0_ref, kk_ref, ka_ref, w2_ref, a2_ref, g2_ref, bd_ref,
                      r_ref, w_ref, k_ref, v_ref, kk_o_ref, b_ref, g_ref, *, c, lw, la, tiles_per_seq):
    p = p_ref[...]
    if tiles_per_seq is None:
        pp = pp_ref[...]
    else:
        first = pl.program_id(0) % tiles_per_seq == 0
        prev_row = jnp.where(first, 0.0, pp_ref[SUBLANES - 1:SUBLANES, :])
        rowid = lax.broadcasted_iota(jnp.int32, p.shape, 0)
        pp = jnp.where(rowid == 0, prev_row, pltpu.roll(p, 1, axis=0))
    xm = p + (pp - p) * mu_ref[...]
    r = xm[:, 0:c]
    k = xm[:, c:2 * c]
    v = xm[:, 2 * c:3 * c]
    o = 3 * c
    wl = xm[:, o:o + lw]
    al = xm[:, o + lw:o + lw + la]
    gl = xm[:, o + lw + la:]
    z = w0_ref[...] + jnp.dot(jnp.tanh(wl).astype(BF16), w2_ref[...], preferred_element_type=F32)
    w = -_softplus(-z) - 0.5
    a = jax.nn.sigmoid(a0_ref[...] + jnp.dot(al.astype(BF16), a2_ref[...], preferred_element_type=F32))
    g = jnp.dot(jax.nn.sigmoid(gl).astype(BF16), g2_ref[...], preferred_element_type=F32)
    kk = k * kk_ref[...]
    decay = jnp.exp(-jnp.exp(w))
    k2 = k * (1.0 + (a - 1.0) * ka_ref[...])
    bd2 = bd_ref[...]
    for s in range(c // LANES):
        sl = slice(s * LANES, (s + 1) * LANES)
        kks = kk[:, sl]
        ss = _seg_dot(kks * kks, bd2)
        kkn = kks / jnp.maximum(jnp.sqrt(ss), 1e-12)
        kk_o_ref[s] = kkn
        b_ref[s] = kkn * a[:, sl]
        r_ref[s] = r[:, sl]
        w_ref[s] = decay[:, sl]
        k_ref[s] = k2[:, sl]
        v_ref[s] = v[:, sl]
        g_ref[s] = g[:, sl]


def _head_pair_indicator(n):
    i = jnp.arange(LANES)
    bd = (i[:, None] // n == i[None, :] // n).astype(BF16)
    return jnp.concatenate([bd, bd], axis=0)


def _rwkv_prep(cfg, p, p_prev, prm, rows_per_seq=None):
    m = p.shape[0]
    c = cfg.rwkv_width
    pw = p.shape[1]
    lw, la = _rup(cfg.lora_w, LANES), _rup(cfg.lora_a, LANES)
    lg = pw - 3 * c - lw - la
    tm = _tile(m if p_prev is not None else rows_per_seq, 128)
    full = lambda i: (0, 0)
    rowblk = lambda i: (i, 0)
    out = jax.ShapeDtypeStruct((c // LANES, m, LANES), F32)
    if p_prev is None:
        assert tm % SUBLANES == 0
        prev_spec = pl.BlockSpec((SUBLANES, pw), lambda i: (jnp.maximum(i * (tm // SUBLANES) - 1, 0), 0))
        p_prev, tiles_per_seq = p, rows_per_seq // tm
    else:
        prev_spec, tiles_per_seq = pl.BlockSpec((tm, pw), rowblk), None
    return pl.pallas_call(
        functools.partial(_rwkv_prep_kernel, c=c, lw=lw, la=la, tiles_per_seq=tiles_per_seq),
        out_shape=(out,) * 7,
        grid=(m // tm,),
        in_specs=[pl.BlockSpec((tm, pw), rowblk), prev_spec,
                  pl.BlockSpec((1, pw), full), pl.BlockSpec((1, c), full), pl.BlockSpec((1, c), full),
                  pl.BlockSpec((1, c), full), pl.BlockSpec((1, c), full),
                  pl.BlockSpec((lw, c), full), pl.BlockSpec((la, c), full), pl.BlockSpec((lg, c), full),
                  pl.BlockSpec((2 * LANES, LANES), full)],
        out_specs=(pl.BlockSpec((c // LANES, tm, LANES), lambda i: (0, i, 0)),) * 7,
        compiler_params=_params("parallel"),
        name="rwkv_prep",
    )(p, p_prev, prm["mu"], prm["w0"], prm["a0"], prm["kk"], prm["ka"], prm["w2"], prm["a2"], prm["g2"],
      _head_pair_indicator(cfg.rwkv_head_dim))


def _rwkv_scan_kernel(r_ref, w_ref, k_ref, v_ref, kk_ref, b_ref, s0_ref, bd_ref,
                      y_ref, sf_ref, yscs_ref, vts_ref, *st_refs, n_pairs, tc):
    t_blk = pl.program_id(2)

    @pl.when(t_blk == 0)
    def _():
        for p in range(n_pairs):
            st_refs[p][...] = s0_ref[0, p]

    rows = min(tc, SUBLANES)
    n = st_refs[0].shape[0]

    nt = (((1,), (1,)), ((), ()))
    row8 = lax.broadcasted_iota(jnp.int32, (SUBLANES, LANES), 0)
    lane8 = lax.broadcasted_iota(jnp.int32, (SUBLANES, LANES), 1)
    rmask = ((lane8 // n == row8 % 2) & (row8 < 4)).astype(F32)
    lane_n = lax.broadcasted_iota(jnp.int32, (n, LANES), 1)

    def group(t0, slot):
        vt_ref, ysc_ref = vts_ref.at[slot], yscs_ref.at[slot]
        vs = [v_ref[p, pl.ds(t0, rows), :] for p in range(n_pairs)]
        pad = LANES - n_pairs * rows
        vt_ref[...] = jnp.transpose(
            jnp.concatenate(vs + ([jnp.zeros((pad, LANES), F32)] if pad else []), axis=0))
        for i in range(rows):
            for p in range(n_pairs):
                row = lambda ref: ref[p, pl.ds(t0 + i, 1), :]
                s = st_refs[p][...]
                z = _seg_dot(s * row(kk_ref), bd_ref[...])
                c = p * rows + i
                vb = jnp.where(lane_n < n, jnp.broadcast_to(vt_ref[0:n, c:c + 1], (n, LANES)),
                               jnp.broadcast_to(vt_ref[n:2 * n, c:c + 1], (n, LANES)))
                s = s * row(w_ref) - z * row(b_ref) + vb * row(k_ref)
                st_refs[p][...] = s
                r = row(r_ref)
                r_hi = r.astype(BF16).astype(F32)
                rm = (jnp.where(row8 < 2, r_hi, r - r_hi) * rmask).astype(BF16)
                y8 = lax.dot_general(rm, s.astype(BF16), nt, preferred_element_type=F32)
                y8 = y8[0:2] + y8[2:4]
                for hh in range(2):
                    ysc_ref[p, hh * rows + i:hh * rows + i + 1, :] = y8[hh:hh + 1]
        for p in range(n_pairs):
            y_ref[p, pl.ds(t0, rows), :] = jnp.concatenate(
                [ysc_ref[p, 0:rows, :], ysc_ref[p, rows:2 * rows, :]], axis=1)

    if tc <= SUBLANES:
        group(0, 0)
    else:
        def body(g, carry):
            t0 = pl.multiple_of(g * 2 * SUBLANES, 2 * SUBLANES)
            group(t0, 0)
            group(t0 + SUBLANES, 1)
            return carry
        lax.fori_loop(0, tc // (2 * SUBLANES), body, 0)

    @pl.when(t_blk == pl.num_programs(2) - 1)
    def _():
        for p in range(n_pairs):
            sf_ref[0, p] = st_refs[p][...]


def _rwkv_scan(cfg, r, w, k, v, kk, b, s0, n_seq, t_len):
    c = cfg.rwkv_width
    n = cfg.rwkv_head_dim
    n_pairs_total = c // LANES
    pg = min(16, n_pairs_total)
    tc = _tile(t_len, 128)
    assert tc <= SUBLANES or tc % (2 * SUBLANES) == 0
    bd2 = _head_pair_indicator(n)
    seq = pl.BlockSpec((pg, None, tc, LANES), lambda bi, g, t: (g, bi, t, 0))
    shp = lambda a: a.reshape(n_pairs_total, n_seq, t_len, LANES)
    st = pl.BlockSpec((1, pg, n, LANES), lambda bi, g, t: (bi, g, 0, 0))
    full = lambda bi, g, t: (0, 0)
    y, sf = pl.pallas_call(
        functools.partial(_rwkv_scan_kernel, n_pairs=pg, tc=tc),
        out_shape=(jax.ShapeDtypeStruct((n_pairs_total, n_seq, t_len, LANES), F32),
                   jax.ShapeDtypeStruct((n_seq, n_pairs_total, n, LANES), F32)),
        grid=(n_seq, n_pairs_total // pg, t_len // tc),
        in_specs=[seq] * 6 + [st, pl.BlockSpec((2 * LANES, LANES), full)],
        out_specs=(seq, st),
        scratch_shapes=[pltpu.VMEM((2, pg, 2 * SUBLANES, n), F32), pltpu.VMEM((2, LANES, LANES), F32)]
        + [pltpu.VMEM((n, LANES), F32)] * pg,
        compiler_params=_params("parallel", "parallel", "arbitrary"),
        name="rwkv_scan",
    )(shp(r), shp(w), shp(k), shp(v), shp(kk), shp(b), s0, bd2)
    return y.reshape(n_pairs_total, n_seq * t_len, LANES), sf


def _rwkv_post_kernel(y_ref, r_ref, k_ref, v_ref, g_ref, lnw_ref, lnb_ref, rk_ref, bd_ref, o_ref, *, n, eps):
    bd2 = bd_ref[...]
    for s in range(y_ref.shape[0]):
        sl = slice(s * LANES, (s + 1) * LANES)
        y = y_ref[s]
        mean = _seg_dot(y, bd2) / n
        d = y - mean
        var = _seg_dot(d * d, bd2) / n
        yn = d * lax.rsqrt(var + eps) * lnw_ref[:, sl] + lnb_ref[:, sl]
        bonus = _seg_dot(r_ref[s] * k_ref[s] * rk_ref[:, sl], bd2)
        yn = yn + bonus * v_ref[s]
        o_ref[:, sl] = (yn * g_ref[s]).astype(o_ref.dtype)


def _rwkv_post(cfg, y, r, k, v, g, prm):
    npair, m, _ = y.shape
    c = npair * LANES
    tm = _tile(m, 256)
    full = lambda i: (0, 0)
    big = pl.BlockSpec((npair, tm, LANES), lambda i: (0, i, 0))
    vec = pl.BlockSpec((1, c), full)
    return pl.pallas_call(
        functools.partial(_rwkv_post_kernel, n=float(cfg.rwkv_head_dim), eps=cfg.gn_eps),
        out_shape=jax.ShapeDtypeStruct((m, c), BF16),
        grid=(m // tm,),
        in_specs=[big] * 5 + [vec] * 3 + [pl.BlockSpec((2 * LANES, LANES), full)],
        out_specs=pl.BlockSpec((tm, c), lambda i: (i, 0)),
        compiler_params=_params("parallel"),
        name="rwkv_post",
    )(y, r, k, v, g, prm["ln_w"], prm["ln_b"], prm["rk"], _head_pair_indicator(cfg.rwkv_head_dim))


def _diff_lambda(lp, lambda_init):
    s1 = jnp.sum(lp[0:1] * lp[1:2], axis=-1, keepdims=True)
    s2 = jnp.sum(lp[2:3] * lp[3:4], axis=-1, keepdims=True)
    return jnp.exp(s1) - jnp.exp(s2) + lambda_init


def _softmax_rows(s):
    m = jnp.max(s, axis=-1, keepdims=True)
    e = jnp.exp(s - m)
    return e / jnp.sum(e, axis=-1, keepdims=True)


def _diff_prompt_kernel(lp_ref, q_ref, k_ref, v_ref, g_ref, o_ref, *, tq, d, lambda_init, eps):
    lam = _diff_lambda(lp_ref[...], lambda_init)
    s_len = q_ref.shape[0]
    kb = k_ref[...].astype(BF16)
    vb = v_ref[...].astype(BF16)
    scale = d ** -0.5
    nt = (((1,), (1,)), ((), ()))
    for qi in range(s_len // tq):
        nk = (qi + 1) * tq
        q = q_ref[qi * tq:(qi + 1) * tq, :].astype(BF16)
        qpos = qi * tq + lax.broadcasted_iota(jnp.int32, (tq, nk), 0)
        kpos = lax.broadcasted_iota(jnp.int32, (tq, nk), 1)
        ok = kpos <= qpos
        ps = []
        for c in range(2):
            s = lax.dot_general(q[:, c * d:(c + 1) * d], kb[:nk, c * d:(c + 1) * d], nt,
                                preferred_element_type=F32) * scale
            ps.append(_softmax_rows(jnp.where(ok, s, NEG_BIG)))
        attn = ps[0] - lam * ps[1]
        o = jnp.dot(attn.astype(BF16), vb[:nk], preferred_element_type=F32)
        o = o * lax.rsqrt(jnp.mean(o * o, axis=-1, keepdims=True) + eps) * g_ref[...]
        o_ref[qi * tq:(qi + 1) * tq, :] = (o * (1.0 - lambda_init)).astype(o_ref.dtype)


def _diff_prompt(cfg, lp, q, k, v, subln):
    m, wdt = q.shape
    hw = 2 * cfg.diff_head_dim
    s_len = cfg.seq
    blk = pl.BlockSpec((s_len, hw), lambda b, h: (b, h))
    return pl.pallas_call(
        functools.partial(_diff_prompt_kernel, tq=_tile(s_len, 256), d=cfg.diff_head_dim,
                          lambda_init=cfg.lambda_init, eps=cfg.subln_eps),
        out_shape=jax.ShapeDtypeStruct((m, wdt), BF16),
        grid=(m // s_len, wdt // hw),
        in_specs=[pl.BlockSpec(lp.shape, lambda b, h: (0, 0)), blk, blk, blk,
                  pl.BlockSpec((1, hw), lambda b, h: (0, 0))],
        out_specs=blk,
        compiler_params=_params("parallel", "parallel"),
        name="diff_prompt",
    )(lp, q, k, v, subln.reshape(1, hw))


def _diff_scores_kernel(pt_ref, q_ref, *refs, scale):
    del pt_ref
    k_refs, s_ref = refs[:-1], refs[-1]
    q = q_ref[0]
    ps = k_refs[0].shape[1]
    lane = lax.broadcasted_iota(jnp.int32, (q.shape[0], ps), 1)
    for r, k_ref in enumerate(k_refs):
        s = jnp.zeros((q.shape[0], ps), F32)
        for t in range(ps):
            col = jnp.sum(k_ref[0, t] * q, axis=-1, keepdims=True)
            s = jnp.where(lane == t, col, s)
        s_ref[0, :, r * ps:(r + 1) * ps] = s * scale


def _diff_weights_kernel(lp_ref, sa_ref, qa_ref, ka_ref, sb_ref, qb_ref, kb_ref, w_ref, wn_ref, *,
                         scale, lambda_init):
    lam = _diff_lambda(lp_ref[...], lambda_init)
    p, pn = [], []
    for s_r, q_r, k_r in ((sa_ref, qa_ref, ka_ref), (sb_ref, qb_ref, kb_ref)):
        s = s_r[0]
        sn = jnp.sum(q_r[0] * k_r[0], axis=-1, keepdims=True) * scale
        mx = jnp.maximum(jnp.max(s, axis=-1, keepdims=True), sn)
        e = jnp.exp(s - mx)
        en = jnp.exp(sn - mx)
        den = jnp.sum(e, axis=-1, keepdims=True) + en
        p.append(e / den)
        pn.append(en / den)
    w_ref[0] = p[0] - lam * p[1]
    wn_ref[0] = pn[0] - lam * pn[1]


def _diff_pv_kernel(pt_ref, w_ref, wn_ref, vn_ref, rep_ref, subln_ref, *refs, lambda_init, eps):
    del pt_ref
    v_refs, o_ref, acc_ref = refs[:-2], refs[-2], refs[-1]
    pg = pl.program_id(1)

    @pl.when(pg == 0)
    def _():
        acc_ref[...] = wn_ref[0].astype(BF16).astype(F32) * vn_ref[0].astype(BF16).astype(F32)

    _, ps, h, hw = v_refs[0].shape
    row = lax.broadcasted_iota(jnp.int32, (h, ps * h), 0)
    col = lax.broadcasted_iota(jnp.int32, (h, ps * h), 1)
    acc = acc_ref[...]
    for r, v_ref in enumerate(v_refs):
        w = w_ref[0, :, r * ps:(r + 1) * ps].astype(BF16)
        we = jnp.dot(w, rep_ref[...], preferred_element_type=F32)
        wd = jnp.where(col % h == row, we, 0.0).astype(BF16)
        acc = acc + jnp.dot(wd, v_ref[0].reshape(ps * h, hw).astype(BF16), preferred_element_type=F32)
    acc_ref[...] = acc

    @pl.when(pg == pl.num_programs(1) - 1)
    def _():
        o = acc_ref[...]
        ms = jnp.mean(o * o, axis=-1, keepdims=True)
        o_ref[0] = o * lax.rsqrt(ms + eps) * subln_ref[...] * (1.0 - lambda_init)


def _diff_sample(cfg, lp, q, k, v, cache_k, cache_v, page_table, subln):
    db, wdt = q.shape
    h = cfg.diff_heads
    d = cfg.diff_head_dim
    hw = 2 * d
    ps = cfg.page_size
    n_pages = page_table.shape[1]
    n_past = n_pages * ps
    scale = d ** -0.5
    q3, k3 = q.reshape(db, 2 * h, d), k.reshape(db, 2 * h, d)
    v3 = v.reshape(db, h, hw)
    pps = _tile(n_pages, 4)
    rows = lambda nr, nc: pl.BlockSpec((1, nr, nc), lambda b, p, pt: (b, 0, 0))

    def pages(nr, nc):
        return [pl.BlockSpec((1, ps, nr, nc), lambda b, p, pt, r=r: (pt[b, p * pps + r], 0, 0, 0))
                for r in range(pps)]

    s = pl.pallas_call(
        functools.partial(_diff_scores_kernel, scale=scale),
        out_shape=jax.ShapeDtypeStruct((db, 2 * h, n_past), F32),
        grid_spec=pltpu.PrefetchScalarGridSpec(
            num_scalar_prefetch=1, grid=(db, n_pages // pps),
            in_specs=[rows(2 * h, d)] + pages(2 * h, d),
            out_specs=pl.BlockSpec((1, 2 * h, pps * ps), lambda b, p, pt: (b, 0, p))),
        compiler_params=_params("parallel", "arbitrary"),
        name="diff_scores",
    )(page_table, q3, *([cache_k] * pps))
    maps = lambda a: [a.reshape(db, h, 2, a.shape[-1])[:, :, c] for c in range(2)]
    (qa, qb), (ka, kb), (sa, sb) = maps(q3), maps(k3), maps(s)
    headb = pl.BlockSpec((1, h, d), lambda b: (b, 0, 0))
    pastb = pl.BlockSpec((1, h, n_past), lambda b: (b, 0, 0))
    w, wn = pl.pallas_call(
        functools.partial(_diff_weights_kernel, scale=scale, lambda_init=cfg.lambda_init),
        out_shape=(jax.ShapeDtypeStruct((db, h, n_past), F32), jax.ShapeDtypeStruct((db, h, 1), F32)),
        grid=(db,),
        in_specs=[pl.BlockSpec(lp.shape, lambda b: (0, 0)), pastb, headb, headb, pastb, headb, headb],
        out_specs=(pastb, pl.BlockSpec((1, h, 1), lambda b: (b, 0, 0))),
        compiler_params=_params("parallel"),
        name="diff_weights",
    )(lp, sa, qa, ka, sb, qb, kb)
    repeat = (jnp.arange(ps)[:, None] == (jnp.arange(ps * h) // h)[None, :]).astype(BF16)
    o = pl.pallas_call(
        functools.partial(_diff_pv_kernel, lambda_init=cfg.lambda_init, eps=cfg.subln_eps),
        out_shape=jax.ShapeDtypeStruct((db, h, hw), F32),
        grid_spec=pltpu.PrefetchScalarGridSpec(
            num_scalar_prefetch=1, grid=(db, n_pages // pps),
            in_specs=[pl.BlockSpec((1, h, pps * ps), lambda b, p, pt: (b, 0, p)),
                      rows(h, 1), rows(h, hw),
                      pl.BlockSpec((ps, ps * h), lambda b, p, pt: (0, 0)),
                      pl.BlockSpec((1, hw), lambda b, p, pt: (0, 0))] + pages(h, hw),
            out_specs=rows(h, hw),
            scratch_shapes=[pltpu.VMEM((h, hw), F32)]),
        compiler_params=_params("parallel", "arbitrary"),
        name="diff_pv",
    )(page_table, w, wn, v3, repeat, subln.reshape(1, hw), *([cache_v] * pps))
    return o.reshape(db, wdt)


def _top_mask_lanes(gate, n_top):
    nb = gate.shape[1]
    lane = lax.broadcasted_iota(jnp.int32, gate.shape, 1).astype(F32)
    sel = jnp.zeros(gate.shape, F32)
    for _ in range(n_top):
        mx = jnp.max(gate, axis=1, keepdims=True)
        idx = jnp.min(jnp.where(gate == mx, lane, float(nb)), axis=1, keepdims=True)
        hit = (lane == idx) & (mx > 0.5 * NEG_BIG)
        sel = jnp.where(hit, 1.0, sel)
        gate = jnp.where(lane == idx, NEG_BIG, gate)
    return sel


def _moba_prompt_kernel(q_ref, k_ref, v_ref, ext_ref, o_ref, *, bs, n_top):
    s_len, d = q_ref.shape
    nb = s_len // bs
    kf = k_ref[...]
    vb = v_ref[...].astype(BF16)
    kmean = jnp.sum(kf.reshape(nb, bs, d), axis=1) / bs
    kmean = jnp.concatenate([kmean, jnp.zeros((LANES - nb, d), F32)], axis=0)
    k_aug = jnp.concatenate([kf.astype(BF16), ext_ref[...]], axis=1)
    scale = d ** -0.5
    nt = (((1,), (1,)), ((), ()))
    causal = (lax.broadcasted_iota(jnp.int32, (bs, bs), 1) <= lax.broadcasted_iota(jnp.int32, (bs, bs), 0))
    blk = lax.broadcasted_iota(jnp.int32, (bs, LANES), 1)
    gate_all = lax.dot_general(q_ref[...], kmean, nt, precision=lax.Precision.HIGHEST,
                               preferred_element_type=F32)
    for qi in range(nb):
        lo, nk = qi * bs, (qi + 1) * bs
        qf = q_ref[lo:nk, :]
        allowed = blk == qi
        if qi > 0:
            gate = gate_all[lo:nk]
            allowed = allowed | (_top_mask_lanes(jnp.where(blk < qi, gate, NEG_BIG), n_top) > 0.5)
        bias = jnp.where(allowed, 0.0, NEG_BIG).astype(BF16)
        q_aug = jnp.concatenate([(qf * scale).astype(BF16), bias], axis=1)
        s = lax.dot_general(q_aug, k_aug[:nk], nt, preferred_element_type=F32)
        s_own = jnp.where(causal, s[:, lo:nk], NEG_BIG)
        mx = jnp.max(s_own, axis=-1, keepdims=True)
        if qi > 0:
            mx = jnp.maximum(mx, jnp.max(s[:, :lo], axis=-1, keepdims=True))
        e_own = jnp.exp(s_own - mx)
        den = jnp.sum(e_own, axis=-1, keepdims=True)
        o = jnp.dot(e_own.astype(BF16), vb[lo:nk], preferred_element_type=F32)
        if qi > 0:
            e_past = jnp.exp(s[:, :lo] - mx)
            den = den + jnp.sum(e_past, axis=-1, keepdims=True)
            o = o + jnp.dot(e_past.astype(BF16), vb[:lo], preferred_element_type=F32)
        o_ref[lo:nk, :] = (o / den).astype(o_ref.dtype)


def _moba_prompt(cfg, q, k, v):
    m, wdt = q.shape
    d = cfg.moba_head_dim
    s_len = cfg.seq
    bs = cfg.moba_block
    nb = s_len // bs
    assert nb <= LANES
    onehot = ((jnp.arange(s_len) // bs)[:, None] == jnp.arange(LANES)[None, :]).astype(BF16)
    blk = pl.BlockSpec((s_len, d), lambda b, h: (b, h))
    return pl.pallas_call(
        functools.partial(_moba_prompt_kernel, bs=bs, n_top=min(cfg.moba_topk, nb)),
        out_shape=jax.ShapeDtypeStruct((m, wdt), BF16),
        grid=(m // s_len, wdt // d),
        in_specs=[blk, blk, blk, pl.BlockSpec((s_len, LANES), lambda b, h: (0, 0))],
        out_specs=blk,
        compiler_params=_params("parallel", "parallel"),
        name="moba_prompt",
    )(q, k, v, onehot)


def _moba_select_kernel(pt_ref, q_ref, kn_ref, *refs, ppb, bs, n_past_blocks, own_block, n_top):
    del pt_ref
    k_refs, sel_ref, gate_ref = refs[:ppb], refs[ppb], refs[ppb + 1]
    blk = pl.program_id(1)
    q = q_ref[0]
    lane = lax.broadcasted_iota(jnp.int32, gate_ref.shape, 1)

    @pl.when(blk == 0)
    def _():
        gate_ref[...] = jnp.zeros(gate_ref.shape, F32)

    def body(t, tot):
        for r in k_refs:
            tot = tot + r[0, t]
        return tot

    tot = lax.fori_loop(0, k_refs[0].shape[1], body, jnp.zeros(q.shape, F32), unroll=8)
    g = jnp.sum(tot * q, axis=-1, keepdims=True) / bs
    gate_ref[...] = jnp.where(lane == blk, g, gate_ref[...])

    @pl.when(blk == pl.num_programs(1) - 1)
    def _():
        g_own = jnp.sum(kn_ref[0] * q, axis=-1, keepdims=True) / bs
        gate = gate_ref[...] + jnp.where(lane == own_block, g_own, 0.0)
        gate = jnp.where((lane < own_block) & (lane < n_past_blocks), gate, NEG_BIG)
        lanef = lane.astype(F32)
        out = jnp.zeros(gate.shape, F32)
        for j in range(n_top):
            mx = jnp.max(gate, axis=1, keepdims=True)
            idx = jnp.min(jnp.where(gate == mx, lanef, float(LANES)), axis=1, keepdims=True)
            out = jnp.where(lane == j, idx, out)
            gate = jnp.where(lanef == idx, NEG_BIG, gate)
        sel_ref[0] = out.astype(jnp.int32)


def _moba_sample_attn_kernel(sel_ref, pt_ref, q_ref, kn_ref, vn_ref, *refs, n_pg, scale, hg):
    del sel_ref, pt_ref
    k_refs, v_refs, o_ref = refs[:n_pg], refs[n_pg:2 * n_pg], refs[2 * n_pg]
    hh = pl.program_id(1) % hg
    ps = k_refs[0].shape[1]
    q = q_ref[0, 0]
    row = lax.broadcasted_iota(jnp.int32, (hg, ps), 0)
    lane = lax.broadcasted_iota(jnp.int32, (hg, ps), 1)

    def page_scores(r):
        s8 = jnp.zeros((hg, ps), F32)
        for t in range(ps):
            col = jnp.sum(r[0, t] * q, axis=-1, keepdims=True)
            s8 = jnp.where(lane == t, col, s8)
        return jnp.sum(jnp.where(row == hh, s8, 0.0), axis=0, keepdims=True) * scale

    s = [page_scores(r) for r in k_refs]
    sn = jnp.sum(q * kn_ref[0, 0], axis=-1, keepdims=True) * scale
    mx = sn
    for x in s:
        mx = jnp.maximum(mx, jnp.max(x, axis=-1, keepdims=True))
    e = [jnp.exp(x - mx) for x in s]
    en = jnp.exp(sn - mx)
    den = en
    for x in e:
        den = den + jnp.sum(x, axis=-1, keepdims=True)
    n_acc = 4
    accs = [jnp.zeros((hg, q.shape[1]), F32)] * n_acc
    for x, r in zip(e, v_refs):
        for t in range(ps):
            accs[t % n_acc] = accs[t % n_acc] + r[0, t] * x[:, t:t + 1]
    acc = (accs[0] + accs[1]) + (accs[2] + accs[3])
    row_d = lax.broadcasted_iota(jnp.int32, acc.shape, 0)
    o = jnp.sum(jnp.where(row_d == hh, acc, 0.0), axis=0, keepdims=True) + en * vn_ref[0, 0]
    o_ref[0, 0] = o / den


def _moba_sample(cfg, q, k, v, cache_k, cache_v, page_table):
    db, wdt = q.shape
    d = cfg.moba_head_dim
    nh = cfg.moba_heads
    ps, bs = cfg.page_size, cfg.moba_block
    ppb = bs // ps
    n_pages = page_table.shape[1]
    n_pool = cache_k.shape[0]
    assert n_pages % ppb == 0 and cfg.past_len == n_pages * ps
    nbp = n_pages // ppb
    own = cfg.past_len // bs
    n_top = cfg.moba_topk
    assert nbp >= n_top and own == nbp and own < LANES
    hg = min(SUBLANES, nh)
    assert nh % hg == 0
    q3, k3 = q.reshape(db, nh, d), k.reshape(db, nh, d)

    def page_spec(r):
        return pl.BlockSpec((1, ps, nh, d), lambda b, n, pt: (pt[b, n * ppb + r], 0, 0, 0))

    allh = pl.BlockSpec((1, nh, d), lambda b, n, pt: (b, 0, 0))
    sel = pl.pallas_call(
        functools.partial(_moba_select_kernel, ppb=ppb, bs=float(bs), n_past_blocks=nbp, own_block=own,
                          n_top=n_top),
        out_shape=jax.ShapeDtypeStruct((db, nh, LANES), jnp.int32),
        grid_spec=pltpu.PrefetchScalarGridSpec(
            num_scalar_prefetch=1, grid=(db, nbp),
            in_specs=[allh, allh] + [page_spec(r) for r in range(ppb)],
            out_specs=pl.BlockSpec((1, nh, LANES), lambda b, n, pt: (b, 0, 0)),
            scratch_shapes=[pltpu.VMEM((nh, LANES), F32)]),
        compiler_params=_params("parallel", "arbitrary"),
        name="moba_select",
    )(page_table, q3, k3, *([cache_k] * ppb))
    sel_flat = sel[:, :, :n_top].reshape(-1)
    pt_flat = page_table.reshape(-1)
    n_pg = n_top * ppb

    def sel_page(jp):
        j, r = divmod(jp, ppb)
        return pl.BlockSpec(
            (1, ps, hg, d),
            lambda b, h, sl, pt: (pt[b * n_pages + sl[(b * nh + h) * n_top + j] * ppb + r], 0, h // hg, 0))

    q4, k4, v4 = (a.reshape(db, nh, 1, d) for a in (q, k, v))
    headb = pl.BlockSpec((1, 1, 1, d), lambda b, h, sl, pt: (b, h, 0, 0))
    o = pl.pallas_call(
        functools.partial(_moba_sample_attn_kernel, n_pg=n_pg, scale=d ** -0.5, hg=hg),
        out_shape=jax.ShapeDtypeStruct((db, nh, 1, d), F32),
        grid_spec=pltpu.PrefetchScalarGridSpec(
            num_scalar_prefetch=2, grid=(db, nh),
            in_specs=[headb, headb, headb] + [sel_page(jp) for jp in range(n_pg)] * 2,
            out_specs=headb),
        compiler_params=_params("parallel", "parallel"),
        name="moba_sample_attn",
    )(sel_flat, pt_flat, q4, k4, v4, *([cache_k] * n_pg), *([cache_v] * n_pg))
    return o.reshape(db, wdt)


def _rope_tables(cfg, pos, d):
    half = d // 2
    inv = cfg.rope_theta ** (-jnp.arange(half, dtype=F32) / half)
    ang = pos.astype(F32)[:, None] * inv[None, :]
    cos, sin = jnp.cos(ang), jnp.sin(ang)
    return jnp.concatenate([cos, cos], axis=-1), jnp.concatenate([-sin, sin], axis=-1)


def _rwkv_layout(cfg):
    c = cfg.rwkv_width
    segs, o = [], 0
    for wd in (c, c, c, cfg.lora_w, cfg.lora_a, cfg.lora_g):
        segs.append((o, wd, _rup(wd, LANES)))
        o += wd
    return segs, o


def _pad_cols(cfg, a):
    segs, _ = _rwkv_layout(cfg)
    parts = []
    for o, wd, pw in segs:
        x = a[..., o:o + wd]
        if pw != wd:
            x = jnp.pad(x, [(0, 0)] * (a.ndim - 1) + [(0, pw - wd)])
        parts.append(x)
    return jnp.concatenate(parts, axis=-1)


def _unpad_cols(cfg, a):
    segs, _ = _rwkv_layout(cfg)
    parts, o = [], 0
    for _, wd, pw in segs:
        parts.append(a[..., o:o + wd])
        o += pw
    return jnp.concatenate(parts, axis=-1)


def _pad_rows(a, rows):
    return jnp.pad(a, ((0, rows - a.shape[0]), (0, 0)))


def _state_to_pairs(cfg, s):
    b, h, n, _ = s.shape
    return jnp.transpose(s.reshape(b, h // 2, 2, n, n), (0, 1, 3, 2, 4)).reshape(b, h // 2, n, 2 * n)


def _pairs_to_state(cfg, s):
    b, hp, n, _ = s.shape
    return jnp.transpose(s.reshape(b, hp, n, 2, n), (0, 1, 3, 2, 4)).reshape(b, 2 * hp, n, n)


def _forward(cfg, x_prompt, x_sample, cache_k0, cache_v0, cache_k1, cache_v1, page_table,
             state_wkv0, state_shift0, state_conv,
             norm_mix, norm_ffn, norm_final, w_in0, w_out0,
             rwkv_mu, rwkv_w0, rwkv_w2, rwkv_a0, rwkv_a2, rwkv_g2, rwkv_kk, rwkv_ka, rwkv_rk,
             rwkv_ln_w, rwkv_ln_b, diff_lq1, diff_lk1, diff_lq2, diff_lk2, diff_subln,
             w_in1, w_out1, ffn_w_up, ffn_conv_w, ffn_conv_b, ffn_w_down):
    bsz, s_len, dm = x_prompt.shape
    db = x_sample.shape[0]
    c = cfg.rwkv_width
    dw = dm - c
    segs, rproj = _rwkv_layout(cfg)
    n_heads = c // cfg.rwkv_head_dim

    w_r = _pad_cols(cfg, w_in0[:, :rproj]).astype(BF16)
    w_q, w_k, w_v = (w_in0[:, rproj + i * dw:rproj + (i + 1) * dw].astype(BF16) for i in range(3))
    w_out0_b, w_in1_b, w_out1_b, w_up_b = w_out0, w_in1, w_out1, ffn_w_up
    w_down_b = ffn_w_down.astype(BF16)
    lw_p, la_p = _rup(cfg.lora_w, LANES), _rup(cfg.lora_a, LANES)
    row = lambda a: a.reshape(1, -1)
    prm = dict(mu=row(_pad_cols(cfg, rwkv_mu)), w0=row(rwkv_w0), a0=row(rwkv_a0), kk=row(rwkv_kk),
               ka=row(rwkv_ka), w2=_pad_rows(rwkv_w2, lw_p).astype(BF16),
               a2=_pad_rows(rwkv_a2, la_p).astype(BF16),
               g2=_pad_rows(rwkv_g2, _rup(cfg.lora_g, LANES)).astype(BF16),
               ln_w=row(rwkv_ln_w), ln_b=row(rwkv_ln_b), rk=row(rwkv_rk))
    lp = jnp.stack([diff_lq1, diff_lk1, diff_lq2, diff_lk2])
    rope_p = _rope_tables(cfg, jnp.arange(s_len, dtype=jnp.int32), cfg.diff_head_dim)
    rope_s = _rope_tables(cfg, jnp.full((db,), cfg.past_len, jnp.int32), cfg.diff_head_dim)

    xp = x_prompt.reshape(bsz * s_len, dm)
    xs = x_sample.reshape(db, dm)
    out = {}
    conv_p, conv_s = [], []
    for layer in range(2):
        hp = _rmsnorm(xp, norm_mix[layer], cfg.norm_eps, BF16)
        hs = _rmsnorm(xs, norm_mix[layer], cfg.norm_eps, BF16)
        if layer == 0:
            mix = {}
            for tag, h, n_seq, t_len, rope in (("p", hp, bsz, s_len, rope_p), ("s", hs, db, 1, rope_s)):
                p = _matmul(h, w_r)
                if tag == "p":
                    p_prev = None
                    s0 = jnp.zeros((n_seq, n_heads // 2, cfg.rwkv_head_dim, LANES), F32)
                    shift = p.reshape(n_seq, t_len, -1)[:, -1]
                else:
                    p_prev = _pad_cols(cfg, state_shift0)
                    s0 = _state_to_pairs(cfg, state_wkv0)
                    shift = p
                r, w, k, v, kk, b, g = _rwkv_prep(cfg, p, p_prev, prm, rows_per_seq=t_len)
                y, sf = _rwkv_scan(cfg, r, w, k, v, kk, b, s0, n_seq, t_len)
                a_out = _rwkv_post(cfg, y, r, k, v, g, prm)
                q0 = _matmul(h, w_q, rope=rope, rows_per_seq=t_len if tag == "p" else db)
                k0 = _matmul(h, w_k, rope=rope, rows_per_seq=t_len if tag == "p" else db)
                v0 = _matmul(h, w_v)
                if tag == "p":
                    b_out = _diff_prompt(cfg, lp, q0, k0, v0, diff_subln)
                else:
                    b_out = _diff_sample(cfg, lp, q0, k0, v0, cache_k0, cache_v0, page_table,
                                         diff_subln).astype(BF16)
                mix[tag] = (a_out, b_out)
                out["k0" + tag] = k0.reshape(n_seq, t_len, 2 * cfg.diff_heads, cfg.diff_head_dim)
                out["v0" + tag] = v0.reshape(n_seq, t_len, cfg.diff_heads, 2 * cfg.diff_head_dim)
                out["wkv" + tag] = _pairs_to_state(cfg, sf)
                out["shift" + tag] = _unpad_cols(cfg, shift)
            xp = _matmul(mix["p"][0], w_out0_b, x2=mix["p"][1], w2=w_out0_b, w2_row0=c, res=xp)
            xs = _matmul(mix["s"][0], w_out0_b, x2=mix["s"][1], w2=w_out0_b, w2_row0=c, res=xs)
        else:
            qkv = []
            for tag, h, n_seq, t_len, rope in (("p", hp, bsz, s_len, rope_p), ("s", hs, db, 1, rope_s)):
                rps = t_len if tag == "p" else db
                q1 = _matmul(h, w_in1_b, cols=(0, dm), rope=rope, rows_per_seq=rps)
                k1 = _matmul(h, w_in1_b, cols=(dm, dm), rope=rope, rows_per_seq=rps)
                v1 = _matmul(h, w_in1_b, cols=(2 * dm, dm))
                out["k1" + tag] = k1.reshape(n_seq, t_len, cfg.moba_heads, cfg.moba_head_dim)
                out["v1" + tag] = v1.reshape(n_seq, t_len, cfg.moba_heads, cfg.moba_head_dim)
                qkv.append((q1, k1, v1))
            mp = _moba_prompt(cfg, *qkv[0])
            ms = _moba_sample(cfg, *qkv[1], cache_k1, cache_v1, page_table).astype(BF16)
            xp = _matmul(mp, w_out1_b, res=xp)
            xs = _matmul(ms, w_out1_b, res=xs)
        hp = _rmsnorm(xp, norm_ffn[layer], cfg.norm_eps, BF16)
        hs = _rmsnorm(xs, norm_ffn[layer], cfg.norm_eps, BF16)
        fp, cp = _ffn_up(hp, w_up_b, layer, ffn_conv_w[layer], ffn_conv_b[layer], s_len)
        prev = state_conv[layer]
        fs, gs = _ffn_up_step(hs, w_up_b, layer, ffn_conv_w[layer], ffn_conv_b[layer], prev[:, 0], prev[:, 1])
        xp = _matmul(fp, w_down_b, w_lead=layer, res=xp, tm=512, tn=256)
        xs = _matmul(fs, w_down_b, w_lead=layer, res=xs, tm=512, tn=256)
        conv_p.append(cp)
        conv_s.append(jnp.stack([prev[:, 1], gs], axis=1))
    y_prompt = _rmsnorm(xp, norm_final, cfg.norm_eps, F32).reshape(bsz, s_len, dm)
    y_sample = _rmsnorm(xs, norm_final, cfg.norm_eps, F32).reshape(db, 1, dm)
    return (y_prompt, y_sample, out["k0p"], out["v0p"], out["k0s"], out["v0s"],
            out["wkvp"], out["wkvs"], out["shiftp"], out["shifts"],
            out["k1p"], out["v1p"], out["k1s"], out["v1s"], jnp.stack(conv_p), jnp.stack(conv_s))


def kernel(x_prompt, x_sample, cache_k0, cache_v0, cache_k1, cache_v1, page_table, state_wkv0, state_shift0, state_conv, norm_mix, norm_ffn, norm_final, w_in0, w_out0, rwkv_mu, rwkv_w0, rwkv_w2, rwkv_a0, rwkv_a2, rwkv_g2, rwkv_kk, rwkv_ka, rwkv_rk, rwkv_ln_w, rwkv_ln_b, diff_lq1, diff_lk1, diff_lq2, diff_lk2, diff_subln, w_in1, w_out1, ffn_w_up, ffn_conv_w, ffn_conv_b, ffn_w_down):
    return _forward(_CFG, x_prompt, x_sample, cache_k0, cache_v0, cache_k1, cache_v1, page_table,
                    state_wkv0, state_shift0, state_conv, norm_mix, norm_ffn, norm_final, w_in0, w_out0,
                    rwkv_mu, rwkv_w0, rwkv_w2, rwkv_a0, rwkv_a2, rwkv_g2, rwkv_kk, rwkv_ka, rwkv_rk,
                    rwkv_ln_w, rwkv_ln_b, diff_lq1, diff_lk1, diff_lq2, diff_lk2, diff_subln,
                    w_in1, w_out1, ffn_w_up, ffn_conv_w, ffn_conv_b, ffn_w_down)
```
